```python
import jax, jax.numpy as jnp
from jax import lax
import numpy as np

D_MODEL = 2048
BATCH = 2
SEQ = 16384
DEPTH = 1
DEC_BATCH = 8
DEC_SEQ = 2048
PAST_LEN = 128

N_META = 16
MIX_WIDTH = D_MODEL
GLA_WIDTH = MIX_WIDTH // 2
FNET_WIDTH = MIX_WIDTH - GLA_WIDTH
GLA_HEADS = 4
GLA_HEAD_V = GLA_WIDTH // GLA_HEADS
GLA_KEY = GLA_WIDTH // 2
GLA_HEAD_K = GLA_KEY // GLA_HEADS
GATE_RANK = 16
GATE_TAU = 16.0
CHUNK = 64
FNET_GROUPS = 4
FNET_GROUP_W = FNET_WIDTH // FNET_GROUPS
N_GROUPS = 4
EXPERTS_PER_GROUP = 8
N_EXPERTS = N_GROUPS * EXPERTS_PER_GROUP
TOP_K = 2
EXPERT_FF = D_MODEL // 4
MOE_BLOCK = 128
EPS = 1e-6
OFF_Q = 0
OFF_K = OFF_Q + GLA_KEY
OFF_V = OFF_K + GLA_KEY
OFF_G = OFF_V + GLA_WIDTH
OFF_ZF = OFF_G + GLA_WIDTH
OFF_ZB = OFF_ZF + GATE_RANK
OFF_F = OFF_ZB + GATE_RANK
IN_WIDTH = OFF_F + FNET_WIDTH

kernel_name = "hymba_gla_fnet_hiermoe_encoder"


def _rmsnorm(x, w):
    xf = x.astype(jnp.float32)
    y = xf * lax.rsqrt(jnp.mean(xf * xf, axis=-1, keepdims=True) + EPS)
    return (y * w.astype(jnp.float32)).astype(x.dtype)


def _to_chunks(t, n_heads):
    b, _, w = t.shape
    t = jnp.pad(t, ((0, 0), (CHUNK - N_META, 0), (0, 0)))
    n = t.shape[1] // CHUNK
    return t.reshape(b, n, CHUNK, n_heads, w // n_heads).transpose(0, 3, 1, 2, 4)


def _from_chunks(t):
    b, h, n, c, d = t.shape
    t = t.transpose(0, 2, 3, 1, 4).reshape(b, n * c, h, d)
    return t[:, CHUNK - N_META:]


def _gla_scan(q, k, v, log_a, strict):
    bcum = jnp.cumsum(log_a, axis=3)
    blast = bcum[:, :, :, -1:, :]
    q_dec = q * jnp.exp(bcum)
    k_dec = k * jnp.exp(-bcum)
    k_end = k * jnp.exp(blast - bcum)
    scores = jnp.einsum('bhnid,bhnjd->bhnij', q_dec, k_dec)
    mask = jnp.tril(jnp.ones((CHUNK, CHUNK), dtype=bool), k=-1 if strict else 0)
    scores = jnp.where(mask, scores, 0.0)
    o_intra = jnp.einsum('bhnij,bhnjv->bhniv', scores, v)
    chunk_decay = jnp.exp(blast[:, :, :, 0, :])

    def step(state, xs):
        qn, kn, vn, dn = xs
        o = jnp.einsum('bhid,bhdv->bhiv', qn, state)
        state = state * dn[..., None] + jnp.einsum('bhjd,bhjv->bhdv', kn, vn)
        return state, o

    b, h, _, _, dk = q.shape
    dv = v.shape[-1]
    xs = (jnp.moveaxis(q_dec, 2, 0), jnp.moveaxis(k_end, 2, 0),
          jnp.moveaxis(v, 2, 0), jnp.moveaxis(chunk_decay, 2, 0))
    _, o_inter = lax.scan(step, jnp.zeros((b, h, dk, dv), jnp.float32), xs)
    return o_intra + jnp.moveaxis(o_inter, 0, 2)


def _mixer(h, w_in, w_gate_up, b_gate_up, gla_norm_w, w_out):
    b, l, _ = h.shape
    u = jnp.einsum('bld,de->ble', h, w_in).astype(jnp.float32)
    q = u[..., OFF_Q:OFF_K] * (GLA_HEAD_K ** -0.5)
    k = u[..., OFF_K:OFF_V]
    v = u[..., OFF_V:OFF_G]
    g_out = u[..., OFF_G:OFF_ZF]
    z_f = u[..., OFF_ZF:OFF_ZB]
    z_b = u[..., OFF_ZB:OFF_F]
    four = u[..., OFF_F:IN_WIDTH]
    wg = w_gate_up.astype(jnp.float32)
    bg = b_gate_up.astype(jnp.float32)
    log_a_f = jax.nn.log_sigmoid(z_f @ wg[0] + bg[0]) / GATE_TAU
    log_a_b = jax.nn.log_sigmoid(z_b @ wg[1] + bg[1]) / GATE_TAU

    qc = _to_chunks(q, GLA_HEADS)
    kc = _to_chunks(k, GLA_HEADS)
    vc = _to_chunks(v, GLA_HEADS)
    o_f = _gla_scan(qc, kc, vc, _to_chunks(log_a_f, GLA_HEADS), strict=False)
    rev = lambda t: jnp.flip(t, axis=(2, 3))
    o_b = rev(_gla_scan(rev(qc), rev(kc), rev(vc), rev(_to_chunks(log_a_b, GLA_HEADS)), strict=True))
    o = _from_chunks(o_f + o_b)
    o = o * lax.rsqrt(jnp.mean(o * o, axis=-1, keepdims=True) + EPS)
    o = o * gla_norm_w.astype(jnp.float32).reshape(GLA_HEADS, GLA_HEAD_V)
    o = o.reshape(b, l, GLA_WIDTH) * jax.nn.silu(g_out)

    f = four.reshape(b, l, FNET_GROUPS, FNET_GROUP_W)
    f = jnp.fft.fft2(f, axes=(1, 3), norm='ortho').real.reshape(b, l, FNET_WIDTH)

    merged = jnp.concatenate([o, f], axis=-1).astype(h.dtype)
    return jnp.einsum('ble,ed->bld', merged, w_out)


def _hier_moe(h, w_router_group, b_router_group, w_router_expert, b_router_expert,
              w_expert_gate, w_expert_up, w_expert_down):
    t, d = h.shape
    hf = h.astype(jnp.float32)
    grp_logits = hf @ w_router_group.astype(jnp.float32) + b_router_group.astype(jnp.float32)
    grp_prob = jax.nn.softmax(grp_logits, axis=-1)
    grp = jnp.argmax(grp_logits, axis=-1)
    grp_w = jnp.take_along_axis(grp_prob, grp[:, None], axis=-1)
    exp_logits = (hf @ w_router_expert.astype(jnp.float32)
                  + b_router_expert.astype(jnp.float32)).reshape(t, N_GROUPS, EXPERTS_PER_GROUP)
    exp_logits = jnp.take_along_axis(exp_logits, grp[:, None, None], axis=1)[:, 0]
    top_val, top_idx = lax.top_k(exp_logits, TOP_K)
    gate = jax.nn.softmax(top_val, axis=-1) * grp_w
    expert = grp[:, None] * EXPERTS_PER_GROUP + top_idx

    a = t * TOP_K
    e_flat = expert.reshape(-1)
    tok_flat = jnp.repeat(jnp.arange(t, dtype=jnp.int32), TOP_K)
    w_flat = gate.reshape(-1)
    order = jnp.argsort(e_flat)
    e_sorted = e_flat[order]
    counts = jnp.zeros((N_EXPERTS,), jnp.int32).at[e_flat].add(1)
    starts = jnp.cumsum(counts) - counts
    padded = (counts + MOE_BLOCK - 1) // MOE_BLOCK * MOE_BLOCK
    pends = jnp.cumsum(padded)
    pstarts = pends - padded
    dest = pstarts[e_sorted] + (jnp.arange(a, dtype=jnp.int32) - starts[e_sorted])
    n_blocks = -(-(a + N_EXPERTS * (MOE_BLOCK - 1)) // MOE_BLOCK)
    p = n_blocks * MOE_BLOCK
    buf_tok = jnp.zeros((p,), jnp.int32).at[dest].set(tok_flat[order])
    buf_w = jnp.zeros((p,), jnp.float32).at[dest].set(w_flat[order])
    block_expert = jnp.minimum(
        jnp.searchsorted(pends, jnp.arange(n_blocks, dtype=jnp.int32) * MOE_BLOCK, side='right'),
        N_EXPERTS - 1)

    def run_block(args):
        toks, e = args
        xb = h[toks]
        hid = jax.nn.silu(xb @ w_expert_gate[e]) * (xb @ w_expert_up[e])
        return hid @ w_expert_down[e]

    out = lax.map(run_block, (buf_tok.reshape(n_blocks, MOE_BLOCK), block_expert))
    out = (out.reshape(p, d).astype(jnp.float32) * buf_w[:, None]).astype(h.dtype)
    return jnp.zeros_like(h).at[buf_tok].add(out)


def _encoder(x, meta_tokens, norm1_w, w_in, w_gate_up, b_gate_up, gla_norm_w, w_out,
             norm2_w, w_router_group, b_router_group, w_router_expert, b_router_expert,
             w_expert_gate, w_expert_up, w_expert_down, final_norm_w):
    b, s, d = x.shape
    meta = jnp.broadcast_to(meta_tokens.astype(x.dtype)[None], (b, N_META, d))
    h = jnp.concatenate([meta, x], axis=1)
    for layer in range(DEPTH):
        h = h + _mixer(_rmsnorm(h, norm1_w[layer]), w_in[layer], w_gate_up[layer],
                       b_gate_up[layer], gla_norm_w[layer], w_out[layer])
        hn = _rmsnorm(h, norm2_w[layer]).reshape(b * (N_META + s), d)
        h = h + _hier_moe(hn, w_router_group[layer], b_router_group[layer],
                          w_router_expert[layer], b_router_expert[layer],
                          w_expert_gate[layer], w_expert_up[layer],
                          w_expert_down[layer]).reshape(b, N_META + s, d)
    h = _rmsnorm(h, final_norm_w)
    return h[:, N_META:]


def setup_inputs(seed: int = 0) -> dict:
    key = jax.random.key(seed)
    ks = jax.random.split(key, 20)
    f32 = jnp.float32
    nrm = lambda k, shape, scale: jax.random.normal(k, shape, f32) * scale
    return {
        'x_prompt': nrm(ks[0], (BATCH, SEQ, D_MODEL), 1.0),
        'x_sample': nrm(ks[1], (DEC_BATCH, DEC_SEQ, D_MODEL), 1.0),
        'meta_tokens': nrm(ks[2], (N_META, D_MODEL), 1.0),
        'norm1_w': 1.0 + nrm(ks[3], (DEPTH, D_MODEL), 0.02),
        'w_in': nrm(ks[4], (DEPTH, D_MODEL, IN_WIDTH), D_MODEL ** -0.5),
        'w_gate_up': nrm(ks[5], (DEPTH, 2, GATE_RANK, GLA_KEY), GATE_RANK ** -0.5),
        'b_gate_up': nrm(ks[6], (DEPTH, 2, GLA_KEY), 0.1),
        'gla_norm_w': 1.0 + nrm(ks[7], (DEPTH, GLA_WIDTH), 0.02),
        'w_out': nrm(ks[8], (DEPTH, MIX_WIDTH, D_MODEL), MIX_WIDTH ** -0.5),
        'norm2_w': 1.0 + nrm(ks[9], (DEPTH, D_MODEL), 0.02),
        'w_router_group': nrm(ks[10], (DEPTH, D_MODEL, N_GROUPS), D_MODEL ** -0.5),
        'b_router_group': nrm(ks[11], (DEPTH, N_GROUPS), 0.01),
        'w_router_expert': nrm(ks[12], (DEPTH, D_MODEL, N_EXPERTS), D_MODEL ** -0.5),
        'b_router_expert': nrm(ks[13], (DEPTH, N_EXPERTS), 0.01),
        'w_expert_gate': nrm(ks[14], (DEPTH, N_EXPERTS, D_MODEL, EXPERT_FF), D_MODEL ** -0.5),
        'w_expert_up': nrm(ks[15], (DEPTH, N_EXPERTS, D_MODEL, EXPERT_FF), D_MODEL ** -0.5),
        'w_expert_down': nrm(ks[16], (DEPTH, N_EXPERTS, EXPERT_FF, D_MODEL), EXPERT_FF ** -0.5),
        'final_norm_w': 1.0 + nrm(ks[17], (D_MODEL,), 0.02),
    }


def reference(x_prompt, x_sample, meta_tokens, norm1_w, w_in, w_gate_up, b_gate_up, gla_norm_w,
              w_out, norm2_w, w_router_group, b_router_group, w_router_expert, b_router_expert,
              w_expert_gate, w_expert_up, w_expert_down, final_norm_w):
    y_prompt = _encoder(x_prompt, meta_tokens, norm1_w, w_in, w_gate_up, b_gate_up, gla_norm_w,
                        w_out, norm2_w, w_router_group, b_router_group, w_router_expert,
                        b_router_expert, w_expert_gate, w_expert_up, w_expert_down, final_norm_w)
    y_sample = _encoder(x_sample, meta_tokens, norm1_w, w_in, w_gate_up, b_gate_up, gla_norm_w,
                        w_out, norm2_w, w_router_group, b_router_group, w_router_expert,
                        b_router_expert, w_expert_gate, w_expert_up, w_expert_down, final_norm_w)
    return (y_prompt, y_sample)
```

```python
import functools
import math

import numpy as np
import jax
import jax.numpy as jnp
from jax import lax
from jax.experimental import pallas as pl
from jax.experimental.pallas import tpu as pltpu

F32 = jnp.float32
BF16 = jnp.bfloat16
HIGHEST = lax.Precision.HIGHEST

D_MODEL = 2048
N_META = 16
GLA_WIDTH = 1024
FNET_WIDTH = 1024
GLA_HEADS = 4
HEAD_V = 256
GLA_KEY = 512
HEAD_K = 128
GATE_RANK = 16
GATE_TAU = 16.0
CHUNK = 64
FNET_GROUP_W = 256
N_GROUPS = 4
EXPERTS_PER_GROUP = 8
N_EXPERTS = 32
TOP_K = 2
EXPERT_FF = 512
EPS = 1e-6
OFF_K = 512
OFF_V = 1024
OFF_G = 2048
OFF_ZF = 3072
OFF_F = 3104

LANES = 128
MOE_BM = 256
VMEM_LIMIT = 56 * 1024 * 1024


def _cparams(semantics, vmem=VMEM_LIMIT):
    return pltpu.CompilerParams(dimension_semantics=semantics, vmem_limit_bytes=vmem)


def _resident(shape):
    nd = len(shape)
    return pl.BlockSpec(shape, lambda *_: (0,) * nd, pipeline_mode=pl.Buffered(1))


def _row_tile(n, target):
    t = min(n, target)
    while n % t:
        t -= 8
    return t


def _inproj_body(x_ref, fm_ref, nw_ref, wqk_ref, wvg_ref, wz_ref, wf_ref, qk_ref, vg_ref, z_ref, f_ref, *, nt):
    i = pl.program_id(1)

    @pl.when(i < nt)
    def _():
        x = x_ref[0]
        y = x * lax.rsqrt(jnp.mean(x * x, axis=-1, keepdims=True) + EPS)
        yb = (y * nw_ref[...]).astype(BF16)
        qk_ref[0] = jnp.dot(yb, wqk_ref[...], preferred_element_type=F32)
        vg_ref[0] = jnp.dot(yb, wvg_ref[...], preferred_element_type=F32)
        z_ref[0] = jnp.dot(yb, wz_ref[...], preferred_element_type=F32)
        f_ref[0] = jnp.dot(yb, wf_ref[...], preferred_element_type=F32)

    @pl.when(i == nt)
    def _():
        f_ref[0, 0:N_META, :] = fm_ref[0]


def _inproj(x, four_meta, nw, wqk, wvg, wz, wf):
    b, s, d = x.shape
    tm = _row_tile(s, 512)
    nt = s // tm
    extra = 0 if four_meta is None else 1
    if four_meta is None:
        four_meta = jnp.zeros((1, N_META, FNET_WIDTH), F32)
    row = lambda w: pl.BlockSpec((1, tm, w), lambda bi, i: (bi, jnp.minimum(i, nt - 1), 0))
    return pl.pallas_call(
        functools.partial(_inproj_body, nt=nt),
        grid=(b, nt + extra),
        in_specs=[row(d), _resident(four_meta.shape), _resident((1, d)), _resident(wqk.shape),
                  _resident(wvg.shape), _resident(wz.shape), _resident(wf.shape)],
        out_specs=[row(2 * GLA_KEY), row(2 * GLA_WIDTH), row(LANES),
                   pl.BlockSpec((1, tm, FNET_WIDTH), lambda bi, i: (bi, i, 0))],
        out_shape=[jax.ShapeDtypeStruct((b, s, 2 * GLA_KEY), F32),
                   jax.ShapeDtypeStruct((b, s, 2 * GLA_WIDTH), F32),
                   jax.ShapeDtypeStruct((b, s, LANES), F32),
                   jax.ShapeDtypeStruct((b, s + extra * N_META, FNET_WIDTH), F32)],
        compiler_params=_cparams(("parallel", "arbitrary")),
        name="inproj",
    )(x, four_meta, nw, wqk, wvg, wz, wf)


def _log_sigmoid(x):
    return jnp.minimum(x, 0.0) - jnp.log1p(jnp.exp(-jnp.abs(x)))


def _tri(n, reverse):
    row = lax.broadcasted_iota(jnp.int32, (n, n), 0)
    col = lax.broadcasted_iota(jnp.int32, (n, n), 1)
    return (col >= row) if reverse else (col <= row)


def _meta_state_body(k_ref, v_ref, z_ref, wg_ref, bg_ref, s_ref):
    pre = jnp.dot(z_ref[...], wg_ref[...], precision=HIGHEST, preferred_element_type=F32) + bg_ref[...]
    la = _log_sigmoid(pre) * (1.0 / GATE_TAU)
    bc = jnp.dot(_tri(N_META, False).astype(F32), la, precision=HIGHEST, preferred_element_type=F32)
    bl = bc[N_META - 1:N_META]
    ke = (k_ref[...] * jnp.exp(bl - bc)).astype(BF16)
    v = v_ref[...].astype(BF16)
    for h in range(GLA_HEADS):
        s_ref[h] = lax.dot_general(v[:, h * HEAD_V:(h + 1) * HEAD_V], ke[:, h * HEAD_K:(h + 1) * HEAD_K],
                                   (((0,), (0,)), ((), ())), preferred_element_type=F32)


def _meta_state(k_m, v_m, z_m, wg_pad, bg):
    return pl.pallas_call(
        _meta_state_body,
        out_shape=jax.ShapeDtypeStruct((GLA_HEADS, HEAD_V, HEAD_K), F32),
        name="gla_meta_state",
    )(k_m, v_m, z_m, wg_pad, bg)


def _gla_body(*refs, reverse, cb, has_prev):
    if has_prev:
        qk_ref, v_ref, z_ref, wg_ref, bg_ref, s0_ref, prev_ref, o_ref, st_ref = refs
    else:
        qk_ref, v_ref, z_ref, wg_ref, bg_ref, s0_ref, o_ref, st_ref = refs
        prev_ref = None

    @pl.when(pl.program_id(1) == 0)
    def _():
        st_ref[...] = s0_ref[...]

    tri = _tri(CHUNK, reverse)
    trif = tri.astype(F32)
    if reverse:
        row = lax.broadcasted_iota(jnp.int32, (CHUNK, CHUNK), 0)
        col = lax.broadcasted_iota(jnp.int32, (CHUNK, CHUNK), 1)
        msk = col > row
    else:
        msk = tri
    wg = wg_ref[...]
    bg = bg_ref[...]
    scale = HEAD_K ** -0.5
    contract_last = (((1,), (1,)), ((), ()))
    contract_first = (((0,), (0,)), ((), ()))
    order = range(cb - 1, -1, -1) if reverse else range(cb)
    for c in order:
        rows = slice(c * CHUNK, (c + 1) * CHUNK)
        pre = jnp.dot(z_ref[0, rows, :], wg, precision=HIGHEST, preferred_element_type=F32) + bg
        la = _log_sigmoid(pre) * (1.0 / GATE_TAU)
        bc = jnp.dot(trif, la, precision=HIGHEST, preferred_element_type=F32)
        bl = bc[0:1] if reverse else bc[CHUNK - 1:CHUNK]
        q = qk_ref[0, rows, 0:GLA_KEY] * scale
        k = qk_ref[0, rows, GLA_KEY:2 * GLA_KEY]
        qd = (q * jnp.exp(bc)).astype(BF16)
        kd = (k * jnp.exp(-bc)).astype(BF16)
        ke = (k * jnp.exp(bl - bc)).astype(BF16)
        dec = jnp.exp(bl)
        v = v_ref[0, rows, :].astype(BF16)
        for h in range(GLA_HEADS):
            ks = slice(h * HEAD_K, (h + 1) * HEAD_K)
            vs = slice(h * HEAD_V, (h + 1) * HEAD_V)
            sc = lax.dot_general(qd[:, ks], kd[:, ks], contract_last, preferred_element_type=F32)
            sc = jnp.where(msk, sc, 0.0).astype(BF16)
            st = st_ref[h]
            o = (jnp.dot(sc, v[:, vs], preferred_element_type=F32)
                 + lax.dot_general(qd[:, ks], st.astype(BF16), contract_last, preferred_element_type=F32))
            upd = lax.dot_general(v[:, vs], ke[:, ks], contract_first, preferred_element_type=F32)
            st_ref[h] = st * dec[:, ks] + upd
            if prev_ref is not None:
                o = prev_ref[0, rows, vs] + o
            o_ref[0, rows, vs] = o


def _gla(qk, vg, z, wg_pad, bg, s0, prev, reverse):
    b, s, _ = qk.shape
    cb = 4
    rows = cb * CHUNK
    nb = s // rows
    if reverse:
        blk = lambda bi, i: (bi, nb - 1 - i, 0)
    else:
        blk = lambda bi, i: (bi, i, 0)
    in_specs = [pl.BlockSpec((1, rows, 2 * GLA_KEY), blk),
                pl.BlockSpec((1, rows, GLA_WIDTH), blk),
                pl.BlockSpec((1, rows, LANES), blk),
                _resident(wg_pad.shape), _resident(bg.shape), _resident(s0.shape)]
    args = [qk, vg, z, wg_pad, bg, s0]
    if prev is not None:
        in_specs.append(pl.BlockSpec((1, rows, GLA_WIDTH), blk))
        args.append(prev)
    return pl.pallas_call(
        functools.partial(_gla_body, reverse=reverse, cb=cb, has_prev=prev is not None),
        grid=(b, nb),
        in_specs=in_specs,
        out_specs=pl.BlockSpec((1, rows, GLA_WIDTH), blk),
        out_shape=jax.ShapeDtypeStruct((b, s, GLA_WIDTH), F32),
        scratch_shapes=[pltpu.VMEM((GLA_HEADS, HEAD_V, HEAD_K), F32)],
        compiler_params=_cparams(("parallel", "arbitrary")),
        name="gla_bwd" if reverse else "gla_fwd",
    )(*args)


def _dft_factors(length):
    best = None
    for n2 in range(16, length, 16):
        if length % n2 == 0:
            n1 = length // n2
            if best is None or n1 + n2 < best[0] + best[1]:
                best = (n1, n2)
    assert best is not None, length
    return best


def _round_up(x, m):
    return (x + m - 1) // m * m


def _dft_tables(length, n1, n2, n1p):
    two_pi = 2.0 * math.pi
    r = np.arange(n2)
    pa = ((r[:, None] + N_META) * r[None, :]) % n2
    ang = two_pi * pa / n2
    a_stack = np.concatenate([np.cos(ang), -np.sin(ang)], axis=0)
    r1 = np.arange(n1)
    pb = (r1[:, None] * (r1[None, :] + N_META)) % n1
    angb = two_pi * pb / n1
    br = np.zeros((n1p, n1p)); bi = np.zeros((n1p, n1p))
    br[:n1, :n1] = np.cos(angb); bi[:n1, :n1] = -np.sin(angb)
    pt = ((r[:, None] + N_META) * (r1[None, :] + N_META)) % length
    angt = two_pi * pt / length
    twr = np.zeros((n2, 1, n1p)); twi = np.zeros((n2, 1, n1p))
    twr[:, 0, :n1] = np.cos(angt); twi[:, 0, :n1] = -np.sin(angt)
    return (jnp.asarray(a_stack, BF16), jnp.asarray(br, F32), jnp.asarray(bi, F32),
            jnp.asarray(twr, F32), jnp.asarray(twi, F32))


def _bf16_bits(x):
    return pltpu.bitcast(x.astype(BF16).astype(F32), jnp.uint32)


def _seqdft_body(x_ref, a_ref, br_ref, bi_ref, twr_ref, twi_ref, o_ref, ur_ref, ui_ref, *, n1, n2, n1p):
    ur_ref[...] = jnp.zeros(ur_ref.shape, F32)
    ui_ref[...] = jnp.zeros(ui_ref.shape, F32)
    a = a_ref[...]

    def step_a(m1, carry):
        slab = x_ref[0, pl.ds(m1, n2, stride=n1), :].astype(BF16)
        p = jnp.dot(a, slab, preferred_element_type=F32)
        ur_ref[pl.ds(m1, n2, stride=n1p), :] = p[:n2]
        ui_ref[pl.ds(m1, n2, stride=n1p), :] = p[n2:]
        return carry

    lax.fori_loop(0, n1, step_a, 0)

    br = br_ref[...]
    bi = bi_ref[...]

    def step_c(j, carry):
        tr = twr_ref[j]
        ti = twi_ref[j]
        fm = jnp.concatenate([br * tr - bi * ti, br * ti + bi * tr], axis=0).astype(BF16)
        start = pl.multiple_of(j * n1p, 16)
        rhs = jnp.concatenate([ur_ref[pl.ds(start, n1p), :], ui_ref[pl.ds(start, n1p), :]],
                              axis=1).astype(BF16)
        r = jnp.dot(fm, rhs, preferred_element_type=F32)
        vr = r[:n1, :LANES] - r[n1p:n1p + n1, LANES:]
        vi = r[:n1, LANES:] + r[n1p:n1p + n1, :LANES]
        packed = _bf16_bits(vr) | (_bf16_bits(vi) >> 16)
        o_ref[0, pl.ds(j, n1, stride=n2), :] = packed
        return carry

    lax.fori_loop(0, n2, step_c, 0)


def _seqdft(four):
    b, length, c = four.shape
    n1, n2 = _dft_factors(length)
    n1p = _round_up(n1, 16)
    tables = _dft_tables(length, n1, n2, n1p)
    blk = pl.BlockSpec((1, length, LANES), lambda bi, ci: (bi, 0, ci))
    return pl.pallas_call(
        functools.partial(_seqdft_body, n1=n1, n2=n2, n1p=n1p),
        grid=(b, c // LANES),
        in_specs=[pl.BlockSpec((1, length, LANES), lambda bi, ci: (bi, 0, ci), pipeline_mode=pl.Buffered(1))]
                 + [_resident(t.shape) for t in tables],
        out_specs=blk,
        out_shape=jax.ShapeDtypeStruct((b, length, c), jnp.uint32),
        scratch_shapes=[pltpu.VMEM((n2 * n1p, LANES), F32), pltpu.VMEM((n2 * n1p, LANES), F32)],
        compiler_params=_cparams(("parallel", "parallel")),
        name="seqdft",
    )(four, *tables)


def _outproj_body(o_ref, g_ref, v_ref, x_ref, gw_ref, cw_ref, sw_ref, wo_ref, n2w_ref, wr_ref, br_ref,
                  base_in_ref, h_ref, hn_ref, route_ref, cnt_ref, base_ref, *, fnorm):
    first = jnp.logical_and(pl.program_id(0) == 0, pl.program_id(1) == 0)

    @pl.when(first)
    def _():
        base_ref[...] = base_in_ref[...]

    tm = o_ref.shape[1]
    parts = []
    for h in range(GLA_HEADS):
        vs = slice(h * HEAD_V, (h + 1) * HEAD_V)
        o = o_ref[0, :, vs]
        o = o * lax.rsqrt(jnp.mean(o * o, axis=-1, keepdims=True) + EPS)
        o = o * gw_ref[:, vs]
        g = g_ref[0, :, vs]
        parts.append((o * (g * (1.0 / (1.0 + jnp.exp(-g))))).astype(BF16))
    cw = cw_ref[...]
    sw = sw_ref[...]
    for gi in range(FNET_WIDTH // FNET_GROUP_W):
        cs = slice(gi * FNET_GROUP_W, (gi + 1) * FNET_GROUP_W)
        w = v_ref[0, :, cs]
        vr = pltpu.bitcast(w & jnp.uint32(0xFFFF0000), F32).astype(BF16)
        vi = pltpu.bitcast(w << 16, F32).astype(BF16)
        f = (jnp.dot(vr, cw, preferred_element_type=F32) + jnp.dot(vi, sw, preferred_element_type=F32))
        parts.append((f * fnorm).astype(BF16))
    merged = jnp.concatenate(parts, axis=-1)
    h1 = x_ref[0] + jnp.dot(merged, wo_ref[...], preferred_element_type=F32)
    h_ref[0] = h1
    hn = h1 * lax.rsqrt(jnp.mean(h1 * h1, axis=-1, keepdims=True) + EPS) * n2w_ref[...]
    hn_ref[0] = hn

    logits = jnp.dot(hn, wr_ref[...], precision=HIGHEST, preferred_element_type=F32) + br_ref[...]
    lane = lax.broadcasted_iota(jnp.int32, (tm, LANES), 1)
    neg = jnp.float32(-jnp.inf)
    gl = jnp.where(lane < N_GROUPS, logits, neg)
    gmax = jnp.max(gl, axis=-1, keepdims=True)
    grp = jnp.min(jnp.where(gl == gmax, lane, LANES), axis=-1, keepdims=True)
    grp_w = 1.0 / jnp.sum(jnp.exp(gl - gmax), axis=-1, keepdims=True)
    elane = lane - N_GROUPS
    member = jnp.logical_and(jnp.logical_and(elane >= 0, elane < N_EXPERTS),
                             (elane >> 3) == grp)
    el = jnp.where(member, logits, neg)
    m0 = jnp.max(el, axis=-1, keepdims=True)
    i0 = jnp.min(jnp.where(el == m0, lane, LANES), axis=-1, keepdims=True)
    el2 = jnp.where(lane == i0, neg, el)
    m1 = jnp.max(el2, axis=-1, keepdims=True)
    i1 = jnp.min(jnp.where(el2 == m1, lane, LANES), axis=-1, keepdims=True)
    p1 = jnp.exp(m1 - m0)
    den = 1.0 + p1
    w0 = (1.0 / den) * grp_w
    w1 = (p1 / den) * grp_w
    e0 = i0 - N_GROUPS
    e1 = i1 - N_GROUPS

    oh0 = (lane == e0).astype(BF16)
    oh1 = (lane == e1).astype(BF16)
    r_i = lax.broadcasted_iota(jnp.int32, (tm, tm), 0)
    c_i = lax.broadcasted_iota(jnp.int32, (tm, tm), 1)
    ltri = (c_i < r_i).astype(BF16)
    c0 = jnp.dot(ltri, oh0, preferred_element_type=F32)
    c1 = jnp.dot(ltri, oh1, preferred_element_type=F32)
    oh0f = oh0.astype(F32)
    oh1f = oh1.astype(F32)
    tot0 = jnp.sum(oh0f, axis=0, keepdims=True)
    tot1 = jnp.sum(oh1f, axis=0, keepdims=True)
    base = base_ref[...]
    rank0 = jnp.sum(oh0f * (c0 + base), axis=-1, keepdims=True)
    rank1 = jnp.sum(oh1f * (c1 + base + tot0), axis=-1, keepdims=True)
    new_base = base + tot0 + tot1
    base_ref[...] = new_base
    cnt_ref[...] = new_base

    route = jnp.where(lane == 0, e0.astype(F32),
            jnp.where(lane == 1, e1.astype(F32),
            jnp.where(lane == 2, w0,
            jnp.where(lane == 3, w1,
            jnp.where(lane == 4, rank0,
            jnp.where(lane == 5, rank1, 0.0))))))
    route_ref[0] = route


def _outproj(o, vg, vpk, x, gw, cw, sw, wo, n2w, wr, br, base_in, length):
    b, s, d = x.shape
    tm = _row_tile(s, 256)
    fnorm = 1.0 / math.sqrt(length * FNET_GROUP_W)
    row = lambda w, col=0: pl.BlockSpec((1, tm, w), lambda bi, i: (bi, i, col))
    return pl.pallas_call(
        functools.partial(_outproj_body, fnorm=fnorm),
        grid=(b, s // tm),
        in_specs=[row(GLA_WIDTH), row(GLA_WIDTH, 1), row(FNET_WIDTH), row(d),
                  _resident(gw.shape), _resident(cw.shape), _resident(sw.shape), _resident(wo.shape),
                  _resident(n2w.shape), _resident(wr.shape), _resident(br.shape), _resident(base_in.shape)],
        out_specs=[row(d), row(d), row(LANES), pl.BlockSpec((1, LANES), lambda bi, i: (0, 0))],
        out_shape=[jax.ShapeDtypeStruct((b, s, d), F32), jax.ShapeDtypeStruct((b, s, d), F32),
                   jax.ShapeDtypeStruct((b, s, LANES), F32), jax.ShapeDtypeStruct((1, LANES), F32)],
        scratch_shapes=[pltpu.VMEM((1, LANES), F32)],
        compiler_params=_cparams(("arbitrary", "arbitrary")),
        name="outproj_router",
    )(o, vg, vpk, x, gw, cw, sw, wo, n2w, wr, br, base_in)


def _dispatch_body(slot_ref, hn_ref, xs_any, xs_out, sem):
    del xs_any
    tm = hn_ref.shape[0]
    base = pl.program_id(0) * tm

    def copy(r, k):
        s = slot_ref[2 * (base + r) + k]
        return pltpu.make_async_copy(hn_ref.at[pl.ds(r, 1)], xs_out.at[pl.ds(s, 1)], sem)

    def issue(r, carry):
        copy(r, 0).start()
        copy(r, 1).start()
        return carry

    lax.fori_loop(0, tm, issue, 0)

    def drain(r, carry):
        copy(r, 0).wait()
        copy(r, 1).wait()
        return carry

    lax.fori_loop(0, tm, drain, 0)


def _dispatch(slots, hn2d, xs):
    t, d = hn2d.shape
    tm = _row_tile(t, 256)
    return pl.pallas_call(
        _dispatch_body,
        grid_spec=pltpu.PrefetchScalarGridSpec(
            num_scalar_prefetch=1,
            grid=(t // tm,),
            in_specs=[pl.BlockSpec((tm, d), lambda i, sl: (i, 0)),
                      pl.BlockSpec(memory_space=pl.ANY)],
            out_specs=pl.BlockSpec(memory_space=pl.ANY),
            scratch_shapes=[pltpu.SemaphoreType.DMA(())]),
        out_shape=jax.ShapeDtypeStruct(xs.shape, xs.dtype),
        input_output_aliases={2: 0},
        compiler_params=_cparams(("arbitrary",)),
        name="moe_dispatch",
    )(slots, hn2d, xs)


def _experts_body(be_ref, nu_ref, x_ref, wg_ref, wu_ref, wd_ref, y_ref):
    used = pl.program_id(0) < nu_ref[0]

    @pl.when(used)
    def _():
        xb = x_ref[...].astype(BF16)
        gate = jnp.dot(xb, wg_ref[0], preferred_element_type=F32)
        up = jnp.dot(xb, wu_ref[0], preferred_element_type=F32)
        hid = (gate * (1.0 / (1.0 + jnp.exp(-gate))) * up).astype(BF16)
        y_ref[...] = jnp.dot(hid, wd_ref[0], preferred_element_type=F32)

    @pl.when(jnp.logical_not(used))
    def _():
        y_ref[...] = jnp.zeros(y_ref.shape, F32)


def _experts(block_expert, n_used, xs, wg, wu, wd):
    p, d = xs.shape
    nb = p // MOE_BM
    blk = lambda i, be, nu: (jnp.minimum(i, nu[0] - 1), 0)
    wsel = lambda i, be, nu: (be[jnp.minimum(i, nu[0] - 1)], 0, 0)
    return pl.pallas_call(
        _experts_body,
        grid_spec=pltpu.PrefetchScalarGridSpec(
            num_scalar_prefetch=2,
            grid=(nb,),
            in_specs=[pl.BlockSpec((MOE_BM, d), blk),
                      pl.BlockSpec((1, d, EXPERT_FF), wsel),
                      pl.BlockSpec((1, d, EXPERT_FF), wsel),
                      pl.BlockSpec((1, EXPERT_FF, d), wsel)],
            out_specs=pl.BlockSpec((MOE_BM, d), lambda i, be, nu: (i, 0))),
        out_shape=jax.ShapeDtypeStruct((p, d), F32),
        compiler_params=_cparams(("arbitrary",)),
        name="moe_experts",
    )(block_expert, n_used, xs, wg, wu, wd)


def _combine_body(slot_ref, h_ref, route_ref, ys_any, fw_ref, o_ref, y0_ref, y1_ref, sem):
    tm = h_ref.shape[0]
    base = pl.program_id(0) * tm

    def copy(r, k):
        s = slot_ref[2 * (base + r) + k]
        dst = y0_ref if k == 0 else y1_ref
        return pltpu.make_async_copy(ys_any.at[pl.ds(s, 1)], dst.at[pl.ds(r, 1)], sem)

    def issue(r, carry):
        copy(r, 0).start()
        copy(r, 1).start()
        return carry

    lax.fori_loop(0, tm, issue, 0)

    def drain(r, carry):
        copy(r, 0).wait()
        copy(r, 1).wait()
        return carry

    lax.fori_loop(0, tm, drain, 0)

    w0 = route_ref[:, 2:3]
    w1 = route_ref[:, 3:4]
    h = h_ref[...] + (y0_ref[...] * w0 + y1_ref[...] * w1)
    o_ref[...] = h * lax.rsqrt(jnp.mean(h * h, axis=-1, keepdims=True) + EPS) * fw_ref[...]


def _combine(slots, h2d, route2d, ys, fw):
    t, d = h2d.shape
    tm = _row_tile(t, 256)
    return pl.pallas_call(
        _combine_body,
        grid_spec=pltpu.PrefetchScalarGridSpec(
            num_scalar_prefetch=1,
            grid=(t // tm,),
            in_specs=[pl.BlockSpec((tm, d), lambda i, sl: (i, 0)),
                      pl.BlockSpec((tm, LANES), lambda i, sl: (i, 0)),
                      pl.BlockSpec(memory_space=pl.ANY),
                      pl.BlockSpec((1, d), lambda i, sl: (0, 0))],
            out_specs=pl.BlockSpec((tm, d), lambda i, sl: (i, 0)),
            scratch_shapes=[pltpu.VMEM((tm, d), F32), pltpu.VMEM((tm, d), F32),
                            pltpu.SemaphoreType.DMA(())]),
        out_shape=jax.ShapeDtypeStruct((t, d), F32),
        compiler_params=_cparams(("arbitrary",)),
        name="moe_combine",
    )(slots, h2d, route2d, ys, fw)


def _prep_weights(norm1_w, w_in, w_gate_up, b_gate_up, gla_norm_w, w_out, norm2_w, w_router_group,
                  b_router_group, w_router_expert, b_router_expert, w_expert_gate, w_expert_up,
                  w_expert_down, final_norm_w):
    w = w_in[0]
    wz = jnp.zeros((D_MODEL, LANES), F32).at[:, :2 * GATE_RANK].set(w[:, OFF_ZF:OFF_F])
    wg_f = jnp.zeros((LANES, GLA_KEY), F32).at[:GATE_RANK].set(w_gate_up[0, 0])
    wg_b = jnp.zeros((LANES, GLA_KEY), F32).at[GATE_RANK:2 * GATE_RANK].set(w_gate_up[0, 1])
    wr = jnp.zeros((D_MODEL, LANES), F32)
    wr = wr.at[:, :N_GROUPS].set(w_router_group[0]).at[:, N_GROUPS:N_GROUPS + N_EXPERTS].set(w_router_expert[0])
    br = jnp.zeros((1, LANES), F32)
    br = br.at[0, :N_GROUPS].set(b_router_group[0]).at[0, N_GROUPS:N_GROUPS + N_EXPERTS].set(b_router_expert[0])
    k = np.arange(FNET_GROUP_W)
    ang = 2.0 * math.pi * ((k[:, None] * k[None, :]) % FNET_GROUP_W) / FNET_GROUP_W
    return dict(
        nw1=norm1_w[0][None], wqk=w[:, :OFF_V].astype(BF16), wvg=w[:, OFF_V:OFF_ZF].astype(BF16),
        wz=wz.astype(BF16), wf=w[:, OFF_F:].astype(BF16),
        wg_f=wg_f, wg_b=wg_b, bg_f=b_gate_up[0, 0][None], bg_b=b_gate_up[0, 1][None],
        gw=gla_norm_w[0][None], cw=jnp.asarray(np.cos(ang), BF16), sw=jnp.asarray(np.sin(ang), BF16),
        wo=w_out[0].astype(BF16), nw2=norm2_w[0][None], wr=wr, br=br,
        weg=w_expert_gate[0].astype(BF16), weu=w_expert_up[0].astype(BF16), wed=w_expert_down[0].astype(BF16),
        fw=final_norm_w[None])


def _mixer_and_route(x, meta, pw, base_in):
    b, s, d = x.shape
    length = s + N_META
    qk, vg, z, four = _inproj(x, meta["four"], pw["nw1"], pw["wqk"], pw["wvg"], pw["wz"], pw["wf"])
    s0 = _meta_state(meta["k"], meta["v"], meta["z"], pw["wg_f"], pw["bg_f"])
    o_f = _gla(qk, vg, z, pw["wg_f"], pw["bg_f"], s0, None, reverse=False)
    o = _gla(qk, vg, z, pw["wg_b"], pw["bg_b"], jnp.zeros_like(s0), o_f, reverse=True)
    vpk = _seqdft(four)
    return _outproj(o, vg, vpk, x, pw["gw"], pw["cw"], pw["sw"], pw["wo"], pw["nw2"], pw["wr"], pw["br"],
                    base_in, length)


def kernel(x_prompt, x_sample, meta_tokens, norm1_w, w_in, w_gate_up, b_gate_up, gla_norm_w, w_out, norm2_w,
           w_router_group, b_router_group, w_router_expert, b_router_expert, w_expert_gate, w_expert_up,
           w_expert_down, final_norm_w):
    pw = _prep_weights(norm1_w, w_in, w_gate_up, b_gate_up, gla_norm_w, w_out, norm2_w, w_router_group,
                       b_router_group, w_router_expert, b_router_expert, w_expert_gate, w_expert_up,
                       w_expert_down, final_norm_w)
    qk_m, vg_m, z_m, four_m = _inproj(meta_tokens[None], None, pw["nw1"], pw["wqk"], pw["wvg"], pw["wz"],
                                       pw["wf"])
    meta = dict(k=qk_m[0, :, GLA_KEY:], v=vg_m[0, :, :GLA_WIDTH], z=z_m[0], four=four_m)

    xs_in = (x_prompt, x_sample)
    base = jnp.zeros((1, LANES), F32)
    mixed = []
    for x in xs_in:
        h, hn, route, base = _mixer_and_route(x, meta, pw, base)
        mixed.append((h, hn, route))

    counts = base[0, :N_EXPERTS].astype(jnp.int32)
    padded = (counts + MOE_BM - 1) // MOE_BM * MOE_BM
    pends = jnp.cumsum(padded)
    pstart = (pends - padded).astype(F32)
    n_assign = sum(x.shape[0] * x.shape[1] for x in xs_in) * TOP_K
    nb = -(-(n_assign + N_EXPERTS * (MOE_BM - 1)) // MOE_BM)
    n_used = (pends[-1:] // MOE_BM).astype(jnp.int32)
    block_expert = jnp.minimum(
        jnp.searchsorted(pends, jnp.arange(nb, dtype=jnp.int32) * MOE_BM, side="right"),
        N_EXPERTS - 1).astype(jnp.int32)

    slot_list = []
    xs = jnp.zeros((nb * MOE_BM, D_MODEL), F32)
    for (h, hn, route) in mixed:
        t = h.shape[0] * h.shape[1]
        r2 = route.reshape(t, LANES)
        e = r2[:, 0:2].astype(jnp.int32)
        onehot = (e[:, :, None] == jnp.arange(N_EXPERTS, dtype=jnp.int32)).astype(F32)
        slots = (jnp.sum(onehot * pstart, axis=-1) + r2[:, 4:6]).astype(jnp.int32).reshape(-1)
        slot_list.append(slots)
        xs = _dispatch(slots, hn.reshape(t, D_MODEL), xs)

    ys = _experts(block_expert, n_used, xs, pw["weg"], pw["weu"], pw["wed"])

    outs = []
    for x, (h, hn, route), slots in zip(xs_in, mixed, slot_list):
        t = h.shape[0] * h.shape[1]
        y = _combine(slots, h.reshape(t, D_MODEL), route.reshape(t, LANES), ys, pw["fw"])
        outs.append(y.reshape(x.shape))
    return tuple(outs)
```

```python
import functools
import math

import numpy as np
import jax
import jax.numpy as jnp
from jax import lax
from jax.experimental import pallas as pl
from jax.experimental.pallas import tpu as pltpu

F32 = jnp.float32
BF16 = jnp.bfloat16

D_MODEL = 2048
N_META = 16
GLA_WIDTH = 1024
FNET_WIDTH = 1024
GLA_HEADS = 4
HEAD_V = 256
GLA_KEY = 512
HEAD_K = 128
GATE_RANK = 16
GATE_TAU = 16.0
CHUNK = 64
FNET_GROUP_W = 256
N_GROUPS = 4
EXPERTS_PER_GROUP = 8
N_EXPERTS = 32
TOP_K = 2
EXPERT_FF = 512
EPS = 1e-6
OFF_K = 512
OFF_V = 1024
OFF_G = 2048
OFF_ZF = 3072
OFF_F = 3104

LANES = 128
MOE_BM = 256
VMEM_LIMIT = 56 * 1024 * 1024


def _cparams(semantics, vmem=VMEM_LIMIT):
    return pltpu.CompilerParams(dimension_semantics=semantics, vmem_limit_bytes=vmem)


def _resident(shape):
    nd = len(shape)
    return pl.BlockSpec(shape, lambda *_: (0,) * nd, pipeline_mode=pl.Buffered(1))


def _row_tile(n, target):
    t = min(n, target)
    while n % t:
        t -= 8
    return t


def _inproj_body(x_ref, fm_ref, nw_ref, wqk_ref, wvg_ref, wz_ref, wf_ref, qk_ref, vg_ref, z_ref, f_ref, *, nt):
    i = pl.program_id(1)

    @pl.when(i < nt)
    def _():
        x = x_ref[0]
        y = x * lax.rsqrt(jnp.mean(x * x, axis=-1, keepdims=True) + EPS)
        yb = (y * nw_ref[...]).astype(BF16)
        qk_ref[0] = jnp.dot(yb, wqk_ref[...], preferred_element_type=F32)
        vg_ref[0] = jnp.dot(yb, wvg_ref[...], preferred_element_type=F32)
        z_ref[0] = jnp.dot(yb, wz_ref[...], preferred_element_type=F32)
        f_ref[0] = jnp.dot(yb, wf_ref[...], preferred_element_type=F32)

    @pl.when(i == nt)
    def _():
        f_ref[0, 0:N_META, :] = fm_ref[0]


def _inproj(x, four_meta, nw, wqk, wvg, wz, wf):
    b, s, d = x.shape
    tm = _row_tile(s, 512)
    nt = s // tm
    extra = 0 if four_meta is None else 1
    if four_meta is None:
        four_meta = jnp.zeros((1, N_META, FNET_WIDTH), F32)
    row = lambda w: pl.BlockSpec((1, tm, w), lambda bi, i: (bi, jnp.minimum(i, nt - 1), 0))
    return pl.pallas_call(
        functools.partial(_inproj_body, nt=nt),
        grid=(b, nt + extra),
        in_specs=[row(d), _resident(four_meta.shape), _resident((1, d)), _resident(wqk.shape),
                  _resident(wvg.shape), _resident(wz.shape), _resident(wf.shape)],
        out_specs=[row(2 * GLA_KEY), row(2 * GLA_WIDTH), row(LANES),
                   pl.BlockSpec((1, tm, FNET_WIDTH), lambda bi, i: (bi, i, 0))],
        out_shape=[jax.ShapeDtypeStruct((b, s, 2 * GLA_KEY), F32),
                   jax.ShapeDtypeStruct((b, s, 2 * GLA_WIDTH), F32),
                   jax.ShapeDtypeStruct((b, s, LANES), F32),
                   jax.ShapeDtypeStruct((b, s + extra * N_META, FNET_WIDTH), F32)],
        compiler_params=_cparams(("parallel", "arbitrary")),
        name="inproj",
    )(x, four_meta, nw, wqk, wvg, wz, wf)


def _log_sigmoid(x):
    return jnp.minimum(x, 0.0) - jnp.log1p(jnp.exp(-jnp.abs(x)))


def _gate_log_decay(z, wg2, bg):
    n = z.shape[0]
    z_hi = z.astype(BF16)
    z_lo = (z - z_hi.astype(F32)).astype(BF16)
    pp = jnp.dot(jnp.concatenate([z_hi, z_lo], axis=0), wg2, preferred_element_type=F32)
    pre = ((pp[:n, :GLA_KEY] + pp[:n, GLA_KEY:]) + (pp[n:, :GLA_KEY] + pp[n:, GLA_KEY:])) + bg
    return _log_sigmoid(pre) * (1.0 / GATE_TAU)


def _chunk_cumsum(btri, la):
    la_hi = la.astype(BF16)
    la_lo = (la - la_hi.astype(F32)).astype(BF16)
    pp = jnp.dot(btri, jnp.concatenate([la_hi, la_lo], axis=1), preferred_element_type=F32)
    return pp[:, :GLA_KEY] + pp[:, GLA_KEY:]


def _block_tri(rows, reverse):
    r = np.arange(rows)
    same = (r[:, None] // CHUNK) == (r[None, :] // CHUNK)
    tri = (r[None, :] >= r[:, None]) if reverse else (r[None, :] <= r[:, None])
    return jnp.asarray(same & tri, BF16)


def _meta_state_body(k_ref, v_ref, z_ref, wg_ref, bg_ref, tri_ref, s_ref):
    la = _gate_log_decay(z_ref[...], wg_ref[...], bg_ref[...])
    bc = _chunk_cumsum(tri_ref[...], la)
    bl = bc[N_META - 1:N_META]
    ke = (k_ref[...] * jnp.exp(bl - bc)).astype(BF16)
    v = v_ref[...].astype(BF16)
    for h in range(GLA_HEADS):
        s_ref[h] = lax.dot_general(v[:, h * HEAD_V:(h + 1) * HEAD_V], ke[:, h * HEAD_K:(h + 1) * HEAD_K],
                                   (((0,), (0,)), ((), ())), preferred_element_type=F32)


def _meta_state(k_m, v_m, z_m, wg2, bg):
    return pl.pallas_call(
        _meta_state_body,
        out_shape=jax.ShapeDtypeStruct((GLA_HEADS, HEAD_V, HEAD_K), F32),
        name="gla_meta_state",
    )(k_m, v_m, z_m, wg2, bg, _block_tri(N_META, False))


def _gla_body(*refs, reverse, cb, has_prev):
    if has_prev:
        qk_ref, v_ref, z_ref, wg_ref, bg_ref, tri_ref, s0_ref, prev_ref, o_ref, st_ref = refs
    else:
        qk_ref, v_ref, z_ref, wg_ref, bg_ref, tri_ref, s0_ref, o_ref, st_ref = refs
        prev_ref = None

    @pl.when(pl.program_id(1) == 0)
    def _():
        st_ref[...] = s0_ref[...]

    nrow = cb * CHUNK
    row = lax.broadcasted_iota(jnp.int32, (nrow, nrow), 0)
    col = lax.broadcasted_iota(jnp.int32, (nrow, nrow), 1)
    shift = CHUNK.bit_length() - 1
    same_chunk = (row >> shift) == (col >> shift)
    msk = jnp.logical_and(same_chunk, (col > row) if reverse else (col <= row))
    scale = HEAD_K ** -0.5
    contract_last = (((1,), (1,)), ((), ()))
    contract_first = (((0,), (0,)), ((), ()))
    la = _gate_log_decay(z_ref[0], wg_ref[...], bg_ref[...])
    bc = _chunk_cumsum(tri_ref[...], la)
    chunks = [slice(c * CHUNK, (c + 1) * CHUNK) for c in range(cb)]
    edge = [bc[c * CHUNK:c * CHUNK + 1] if reverse else bc[(c + 1) * CHUNK - 1:(c + 1) * CHUNK]
            for c in range(cb)]
    bl = jnp.concatenate([jnp.broadcast_to(e, (CHUNK, GLA_KEY)) for e in edge], axis=0)
    q = qk_ref[0, :, 0:GLA_KEY] * scale
    k = qk_ref[0, :, GLA_KEY:2 * GLA_KEY]
    qd = (q * jnp.exp(bc)).astype(BF16)
    kd = (k * jnp.exp(-bc)).astype(BF16)
    ke = (k * jnp.exp(bl - bc)).astype(BF16)
    dec = [jnp.exp(e) for e in edge]
    v = v_ref[0].astype(BF16)
    order = range(cb - 1, -1, -1) if reverse else range(cb)
    for h in range(GLA_HEADS):
        ks = slice(h * HEAD_K, (h + 1) * HEAD_K)
        vs = slice(h * HEAD_V, (h + 1) * HEAD_V)
        sc = lax.dot_general(qd[:, ks], kd[:, ks], contract_last, preferred_element_type=F32)
        sc = jnp.where(msk, sc, 0.0).astype(BF16)
        o_intra = jnp.dot(sc, v[:, vs], preferred_element_type=F32)
        st = st_ref[h]
        for c in order:
            rows = chunks[c]
            o = o_intra[rows] + lax.dot_general(qd[rows, ks], st.astype(BF16), contract_last,
                                                preferred_element_type=F32)
            upd = lax.dot_general(v[rows, vs], ke[rows, ks], contract_first, preferred_element_type=F32)
            st = st * dec[c][:, ks] + upd
            if prev_ref is not None:
                o = prev_ref[0, rows, vs] + o
            o_ref[0, rows, vs] = o
        st_ref[h] = st


def _gla(qk, vg, z, wg2, bg, s0, prev, reverse):
    b, s, _ = qk.shape
    cb = 4
    rows = cb * CHUNK
    nb = s // rows
    btri = _block_tri(rows, reverse)
    if reverse:
        blk = lambda bi, i: (bi, nb - 1 - i, 0)
    else:
        blk = lambda bi, i: (bi, i, 0)
    in_specs = [pl.BlockSpec((1, rows, 2 * GLA_KEY), blk),
                pl.BlockSpec((1, rows, GLA_WIDTH), blk),
                pl.BlockSpec((1, rows, LANES), blk),
                _resident(wg2.shape), _resident(bg.shape), _resident(btri.shape), _resident(s0.shape)]
    args = [qk, vg, z, wg2, bg, btri, s0]
    if prev is not None:
        in_specs.append(pl.BlockSpec((1, rows, GLA_WIDTH), blk))
        args.append(prev)
    return pl.pallas_call(
        functools.partial(_gla_body, reverse=reverse, cb=cb, has_prev=prev is not None),
        grid=(b, nb),
        in_specs=in_specs,
        out_specs=pl.BlockSpec((1, rows, GLA_WIDTH), blk),
        out_shape=jax.ShapeDtypeStruct((b, s, GLA_WIDTH), F32),
        scratch_shapes=[pltpu.VMEM((GLA_HEADS, HEAD_V, HEAD_K), F32)],
        compiler_params=_cparams(("parallel", "arbitrary")),
        name="gla_bwd" if reverse else "gla_fwd",
    )(*args)


def _dft_factors(length):
    best = None
    for n2 in range(8, length, 8):
        if length % n2 == 0:
            n1 = length // n2
            if best is None or n2 + 2 * n1 < best[1] + 2 * best[0]:
                best = (n1, n2)
    assert best is not None, length
    return best


def _dft_plan(length, channels):
    n1, n2 = _dft_factors(length)
    kc = _round_up(n1, 16)
    pitch = _round_up(n1, 8)
    if (pitch // 8) % 2 == 0:
        pitch += 8
    rows = _round_up(n2 * pitch + max(0, kc - pitch), 8)
    cw = LANES
    assert channels % cw == 0
    group = 2
    return n1, n2, kc, pitch, rows, cw, group


def _round_up(x, m):
    return (x + m - 1) // m * m


def _dft_tables(length, n1, n2, n1p):
    two_pi = 2.0 * math.pi
    r = np.arange(n2)
    pa = ((r[:, None] + N_META) * r[None, :]) % n2
    ang = two_pi * pa / n2
    a_stack = np.concatenate([np.cos(ang), -np.sin(ang)], axis=0)
    r1 = np.arange(n1)
    pb = (r1[:, None] * (r1[None, :] + N_META)) % n1
    angb = two_pi * pb / n1
    br = np.zeros((n1p, n1p)); bi = np.zeros((n1p, n1p))
    br[:n1, :n1] = np.cos(angb); bi[:n1, :n1] = -np.sin(angb)
    pt = ((r[:, None] + N_META) * (r1[None, :] + N_META)) % length
    angt = two_pi * pt / length
    twr = np.zeros((n2, 1, n1p)); twi = np.zeros((n2, 1, n1p))
    twr[:, 0, :n1] = np.cos(angt); twi[:, 0, :n1] = -np.sin(angt)
    return (jnp.asarray(a_stack, BF16), jnp.asarray(br, F32), jnp.asarray(bi, F32),
            jnp.asarray(twr, F32), jnp.asarray(twi, F32))


def _bf16_bits(x):
    return pltpu.bitcast(x.astype(BF16).astype(F32), jnp.uint32)


def _seqdft_body(x_ref, a_ref, br_ref, bi_ref, twr_ref, twi_ref, o_ref, ur_ref, ui_ref, *,
                 n1, n2, kc, pitch, cw, group):
    @pl.when(jnp.logical_and(pl.program_id(0) == 0, pl.program_id(1) == 0))
    def _():
        ur_ref[...] = jnp.zeros(ur_ref.shape, F32)
        ui_ref[...] = jnp.zeros(ui_ref.shape, F32)

    a = a_ref[...]

    def slab_group(m1, cnt):
        slabs = [x_ref[0, pl.ds(m1 + t, n2, stride=n1), :] for t in range(cnt)]
        rhs = (slabs[0] if cnt == 1 else jnp.concatenate(slabs, axis=1)).astype(BF16)
        p = jnp.dot(a, rhs, preferred_element_type=F32)
        for t in range(cnt):
            ur_ref[pl.ds(m1 + t, n2, stride=pitch), :] = p[:n2, t * cw:(t + 1) * cw]
            ui_ref[pl.ds(m1 + t, n2, stride=pitch), :] = p[n2:, t * cw:(t + 1) * cw]

    def step_a(it, carry):
        slab_group(it * group, group)
        return carry

    lax.fori_loop(0, n1 // group, step_a, 0, unroll=2)
    if n1 % group:
        slab_group((n1 // group) * group, n1 % group)

    br = br_ref[...]
    bi = bi_ref[...]

    def step_c(j, carry):
        tr = twr_ref[j]
        ti = twi_ref[j]
        fm = jnp.concatenate([br * tr - bi * ti, br * ti + bi * tr], axis=0).astype(BF16)
        start = pl.multiple_of(j * pitch, 8)
        rhs = jnp.concatenate([ur_ref[pl.ds(start, kc), :], ui_ref[pl.ds(start, kc), :]],
                              axis=1).astype(BF16)
        r = jnp.dot(fm, rhs, preferred_element_type=F32)
        vr = r[:n1, :cw] - r[kc:kc + n1, cw:]
        vi = r[:n1, cw:] + r[kc:kc + n1, :cw]
        o_ref[0, pl.ds(j, n1, stride=n2), :] = _bf16_bits(vr) | (_bf16_bits(vi) >> 16)
        return carry

    lax.fori_loop(0, n2, step_c, 0, unroll=2)


def _seqdft(four):
    b, length, c = four.shape
    n1, n2, kc, pitch, rows, cw, group = _dft_plan(length, c)
    tables = _dft_tables(length, n1, n2, kc)
    return pl.pallas_call(
        functools.partial(_seqdft_body, n1=n1, n2=n2, kc=kc, pitch=pitch, cw=cw, group=group),
        grid=(b, c // cw),
        in_specs=[pl.BlockSpec((1, length, cw), lambda bi, ci: (bi, 0, ci), pipeline_mode=pl.Buffered(1))]
                 + [_resident(t.shape) for t in tables],
        out_specs=pl.BlockSpec((1, length, cw), lambda bi, ci: (bi, 0, ci)),
        out_shape=jax.ShapeDtypeStruct((b, length, c), jnp.uint32),
        scratch_shapes=[pltpu.VMEM((rows, cw), F32), pltpu.VMEM((rows, cw), F32)],
        compiler_params=_cparams(("arbitrary", "arbitrary")),
        name="seqdft",
    )(four, *tables)


def _outproj_body(o_ref, g_ref, v_ref, x_ref, gw_ref, cw_ref, sw_ref, wo_ref, n2w_ref, wr_ref, br_ref,
                  base_in_ref, h_ref, hn_ref, route_ref, cnt_ref, base_ref, *, fnorm):
    first = jnp.logical_and(pl.program_id(0) == 0, pl.program_id(1) == 0)

    @pl.when(first)
    def _():
        base_ref[...] = base_in_ref[...]

    tm = o_ref.shape[1]
    parts = []
    for h in range(GLA_HEADS):
        vs = slice(h * HEAD_V, (h + 1) * HEAD_V)
        o = o_ref[0, :, vs]
        o = o * lax.rsqrt(jnp.mean(o * o, axis=-1, keepdims=True) + EPS)
        o = o * gw_ref[:, vs]
        g = g_ref[0, :, vs]
        parts.append((o * (g * (1.0 / (1.0 + jnp.exp(-g))))).astype(BF16))
    cw = cw_ref[...]
    sw = sw_ref[...]
    for gi in range(FNET_WIDTH // FNET_GROUP_W):
        cs = slice(gi * FNET_GROUP_W, (gi + 1) * FNET_GROUP_W)
        w = v_ref[0, :, cs]
        vr = pltpu.bitcast(w & jnp.uint32(0xFFFF0000), F32).astype(BF16)
        vi = pltpu.bitcast(w << 16, F32).astype(BF16)
        f = (jnp.dot(vr, cw, preferred_element_type=F32) + jnp.dot(vi, sw, preferred_element_type=F32))
        parts.append((f * fnorm).astype(BF16))
    merged = jnp.concatenate(parts, axis=-1)
    h1 = x_ref[0] + jnp.dot(merged, wo_ref[...], preferred_element_type=F32)
    h_ref[0] = h1
    hn = h1 * lax.rsqrt(jnp.mean(h1 * h1, axis=-1, keepdims=True) + EPS) * n2w_ref[...]
    hn_ref[0] = hn

    hn_hi = hn.astype(BF16)
    hn_lo = (hn - hn_hi.astype(F32)).astype(BF16)
    pp = jnp.dot(jnp.concatenate([hn_hi, hn_lo], axis=0), wr_ref[...], preferred_element_type=F32)
    logits = ((pp[:tm, :LANES] + pp[:tm, LANES:]) + (pp[tm:, :LANES] + pp[tm:, LANES:])) + br_ref[...]
    lane = lax.broadcasted_iota(jnp.int32, (tm, LANES), 1)
    neg = jnp.float32(-jnp.inf)
    gl = jnp.where(lane < N_GROUPS, logits, neg)
    gmax = jnp.max(gl, axis=-1, keepdims=True)
    grp = jnp.min(jnp.where(gl == gmax, lane, LANES), axis=-1, keepdims=True)
    grp_w = 1.0 / jnp.sum(jnp.exp(gl - gmax), axis=-1, keepdims=True)
    elane = lane - N_GROUPS
    member = jnp.logical_and(jnp.logical_and(elane >= 0, elane < N_EXPERTS),
                             (elane >> 3) == grp)
    el = jnp.where(member, logits, neg)
    m0 = jnp.max(el, axis=-1, keepdims=True)
    i0 = jnp.min(jnp.where(el == m0, lane, LANES), axis=-1, keepdims=True)
    el2 = jnp.where(lane == i0, neg, el)
    m1 = jnp.max(el2, axis=-1, keepdims=True)
    i1 = jnp.min(jnp.where(el2 == m1, lane, LANES), axis=-1, keepdims=True)
    p1 = jnp.exp(m1 - m0)
    den = 1.0 + p1
    w0 = (1.0 / den) * grp_w
    w1 = (p1 / den) * grp_w
    e0 = i0 - N_GROUPS
    e1 = i1 - N_GROUPS

    oh0 = (lane == e0).astype(BF16)
    oh1 = (lane == e1).astype(BF16)
    r_i = lax.broadcasted_iota(jnp.int32, (tm, tm), 0)
    c_i = lax.broadcasted_iota(jnp.int32, (tm, tm), 1)
    ltri = (c_i < r_i).astype(BF16)
    c0 = jnp.dot(ltri, oh0, preferred_element_type=F32)
    c1 = jnp.dot(ltri, oh1, preferred_element_type=F32)
    oh0f = oh0.astype(F32)
    oh1f = oh1.astype(F32)
    tot0 = jnp.sum(oh0f, axis=0, keepdims=True)
    tot1 = jnp.sum(oh1f, axis=0, keepdims=True)
    base = base_ref[...]
    rank0 = jnp.sum(oh0f * (c0 + base), axis=-1, keepdims=True)
    rank1 = jnp.sum(oh1f * (c1 + base + tot0), axis=-1, keepdims=True)
    new_base = base + tot0 + tot1
    base_ref[...] = new_base
    cnt_ref[...] = new_base

    route = jnp.where(lane == 0, e0.astype(F32),
            jnp.where(lane == 1, e1.astype(F32),
            jnp.where(lane == 2, w0,
            jnp.where(lane == 3, w1,
            jnp.where(lane == 4, rank0,
            jnp.where(lane == 5, rank1, 0.0))))))
    route_ref[0] = route


def _outproj(o, vg, vpk, x, gw, cw, sw, wo, n2w, wr, br, base_in, length):
    b, s, d = x.shape
    tm = _row_tile(s, 256)
    fnorm = 1.0 / math.sqrt(length * FNET_GROUP_W)
    row = lambda w, col=0: pl.BlockSpec((1, tm, w), lambda bi, i: (bi, i, col))
    return pl.pallas_call(
        functools.partial(_outproj_body, fnorm=fnorm),
        grid=(b, s // tm),
        in_specs=[row(GLA_WIDTH), row(GLA_WIDTH, 1), row(FNET_WIDTH), row(d),
                  _resident(gw.shape), _resident(cw.shape), _resident(sw.shape), _resident(wo.shape),
                  _resident(n2w.shape), _resident(wr.shape), _resident(br.shape), _resident(base_in.shape)],
        out_specs=[row(d), row(d), row(LANES), pl.BlockSpec((1, LANES), lambda bi, i: (0, 0))],
        out_shape=[jax.ShapeDtypeStruct((b, s, d), F32), jax.ShapeDtypeStruct((b, s, d), F32),
                   jax.ShapeDtypeStruct((b, s, LANES), F32), jax.ShapeDtypeStruct((1, LANES), F32)],
        scratch_shapes=[pltpu.VMEM((1, LANES), F32)],
        compiler_params=_cparams(("arbitrary", "arbitrary")),
        name="outproj_router",
    )(o, vg, vpk, x, gw, cw, sw, wo, n2w, wr, br, base_in)


DISPATCH_WINDOW = 256


def _dispatch_body(slot_ref, zblk_ref, zval_ref, *refs, counts, win):
    srcs = refs[:len(counts)]
    xs_out, zero_ref, zsem, sems = refs[len(counts):]
    step = pl.program_id(0)
    nsteps = sum(counts) // win

    @pl.when(step == 0)
    def _():
        zero_ref[...] = jnp.zeros(zero_ref.shape, F32)

        def zcopy(i):
            row = pl.multiple_of(zblk_ref[i] * MOE_BM, MOE_BM)
            return pltpu.make_async_copy(zero_ref, xs_out.at[pl.ds(row, MOE_BM)], zsem)

        for i in range(2 * N_EXPERTS):
            @pl.when(zval_ref[i] != 0)
            def _():
                zcopy(i).start()
        for i in range(2 * N_EXPERTS):
            @pl.when(zval_ref[i] != 0)
            def _():
                zcopy(i).wait()

    def for_window(w, start):
        lo = 0
        for src, n in zip(srcs, counts):
            hi = lo + n // win

            @pl.when(jnp.logical_and(w >= lo, w < hi))
            def _(src=src, lo=lo):
                row0 = (w - lo) * win
                tok0 = w * win
                sem = sems.at[lax.rem(w, 2)]

                def copy(r, k):
                    s = slot_ref[2 * (tok0 + r) + k]
                    return pltpu.make_async_copy(src.at[pl.ds(row0 + r, 1)], xs_out.at[pl.ds(s, 1)], sem)

                def body(r, c):
                    for k in range(TOP_K):
                        if start:
                            copy(r, k).start()
                        else:
                            copy(r, k).wait()
                    return c

                lax.fori_loop(0, win, body, 0, unroll=8)

            lo = hi

    for_window(step, True)

    @pl.when(step > 0)
    def _():
        for_window(step - 1, False)

    @pl.when(step == nsteps - 1)
    def _():
        for_window(step, False)


def _dispatch(slots, zblk, zval, sources, n_rows):
    d = sources[0].shape[1]
    counts = tuple(s.shape[0] for s in sources)
    win = functools.reduce(math.gcd, counts + (DISPATCH_WINDOW,))
    return pl.pallas_call(
        functools.partial(_dispatch_body, counts=counts, win=win),
        grid_spec=pltpu.PrefetchScalarGridSpec(
            num_scalar_prefetch=3,
            grid=(sum(counts) // win,),
            in_specs=[pl.BlockSpec(memory_space=pl.ANY) for _ in sources],
            out_specs=pl.BlockSpec(memory_space=pl.ANY),
            scratch_shapes=[pltpu.VMEM((MOE_BM, d), F32), pltpu.SemaphoreType.DMA(()),
                            pltpu.SemaphoreType.DMA((2,))]),
        out_shape=jax.ShapeDtypeStruct((n_rows, d), F32),
        compiler_params=_cparams(("arbitrary",)),
        name="moe_dispatch",
    )(slots, zblk, zval, *sources)


def _experts_body(be_ref, nu_ref, x_ref, wg_ref, wu_ref, wd_ref, y_ref, wgb_ref, wub_ref, wdb_ref):
    i = pl.program_id(0)
    used = i < nu_ref[0]
    ic = jnp.minimum(i, nu_ref[0] - 1)
    fresh = jnp.logical_or(i == 0, be_ref[ic] != be_ref[jnp.maximum(ic - 1, 0)])

    @pl.when(jnp.logical_and(used, fresh))
    def _():
        wgb_ref[...] = wg_ref[0].astype(BF16)
        wub_ref[...] = wu_ref[0].astype(BF16)
        wdb_ref[...] = wd_ref[0].astype(BF16)

    @pl.when(used)
    def _():
        xb = x_ref[...].astype(BF16)
        gate = jnp.dot(xb, wgb_ref[...], preferred_element_type=F32)
        up = jnp.dot(xb, wub_ref[...], preferred_element_type=F32)
        hid = (gate * (1.0 / (1.0 + jnp.exp(-gate))) * up).astype(BF16)
        y_ref[...] = jnp.dot(hid, wdb_ref[...], preferred_element_type=F32)

    @pl.when(jnp.logical_not(used))
    def _():
        y_ref[...] = jnp.zeros(y_ref.shape, F32)


def _experts(block_expert, n_used, xs, wg, wu, wd):
    p, d = xs.shape
    nb = p // MOE_BM
    blk = lambda i, be, nu: (jnp.minimum(i, nu[0] - 1), 0)
    wsel = lambda i, be, nu: (be[jnp.minimum(i, nu[0] - 1)], 0, 0)
    return pl.pallas_call(
        _experts_body,
        grid_spec=pltpu.PrefetchScalarGridSpec(
            num_scalar_prefetch=2,
            grid=(nb,),
            in_specs=[pl.BlockSpec((MOE_BM, d), blk),
                      pl.BlockSpec((1, d, EXPERT_FF), wsel),
                      pl.BlockSpec((1, d, EXPERT_FF), wsel),
                      pl.BlockSpec((1, EXPERT_FF, d), wsel)],
            out_specs=pl.BlockSpec((MOE_BM, d), lambda i, be, nu: (i, 0)),
            scratch_shapes=[pltpu.VMEM((d, EXPERT_FF), BF16), pltpu.VMEM((d, EXPERT_FF), BF16),
                            pltpu.VMEM((EXPERT_FF, d), BF16)]),
        out_shape=jax.ShapeDtypeStruct((p, d), F32),
        compiler_params=_cparams(("arbitrary",)),
        name="moe_experts",
    )(block_expert, n_used, xs, wg, wu, wd)


def _combine_body(slot_ref, h_ref, route_ref, ys_any, fw_ref, o_ref, ybuf_ref, sems, *, nt):
    tm = h_ref.shape[0]
    i = pl.program_id(0)

    def copy(tile, r, k):
        buf = lax.rem(tile, 2)
        s = slot_ref[2 * (tile * tm + r) + k]
        return pltpu.make_async_copy(ys_any.at[pl.ds(s, 1)], ybuf_ref.at[buf, k, pl.ds(r, 1)], sems.at[buf])

    def issue_tile(tile):
        def body(r, carry):
            copy(tile, r, 0).start()
            copy(tile, r, 1).start()
            return carry
        lax.fori_loop(0, tm, body, 0, unroll=8)

    @pl.when(i == 0)
    def _():
        issue_tile(i)

    @pl.when(i + 1 < nt)
    def _():
        issue_tile(i + 1)

    def drain(r, carry):
        copy(i, r, 0).wait()
        copy(i, r, 1).wait()
        return carry

    lax.fori_loop(0, tm, drain, 0, unroll=8)

    buf = lax.rem(i, 2)
    w0 = route_ref[:, 2:3]
    w1 = route_ref[:, 3:4]
    h = h_ref[...] + (ybuf_ref[buf, 0] * w0 + ybuf_ref[buf, 1] * w1)
    o_ref[...] = h * lax.rsqrt(jnp.mean(h * h, axis=-1, keepdims=True) + EPS) * fw_ref[...]


def _combine(slots, h2d, route2d, ys, fw):
    t, d = h2d.shape
    tm = _row_tile(t, 256)
    nt = t // tm
    return pl.pallas_call(
        functools.partial(_combine_body, nt=nt),
        grid_spec=pltpu.PrefetchScalarGridSpec(
            num_scalar_prefetch=1,
            grid=(nt,),
            in_specs=[pl.BlockSpec((tm, d), lambda i, sl: (i, 0)),
                      pl.BlockSpec((tm, LANES), lambda i, sl: (i, 0)),
                      pl.BlockSpec(memory_space=pl.ANY),
                      pl.BlockSpec((1, d), lambda i, sl: (0, 0))],
            out_specs=pl.BlockSpec((tm, d), lambda i, sl: (i, 0)),
            scratch_shapes=[pltpu.VMEM((2, TOP_K, tm, d), F32), pltpu.SemaphoreType.DMA((2,))]),
        out_shape=jax.ShapeDtypeStruct((t, d), F32),
        compiler_params=_cparams(("arbitrary",)),
        name="moe_combine",
    )(slots, h2d, route2d, ys, fw)


def _prep_weights(norm1_w, w_in, w_gate_up, b_gate_up, gla_norm_w, w_out, norm2_w, w_router_group,
                  b_router_group, w_router_expert, b_router_expert, w_expert_gate, w_expert_up,
                  w_expert_down, final_norm_w):
    w = w_in[0]
    wz = jnp.pad(w[:, OFF_ZF:OFF_F], ((0, 0), (0, LANES - 2 * GATE_RANK)))
    wg_f = jnp.pad(w_gate_up[0, 0], ((0, LANES - GATE_RANK), (0, 0)))
    wg_b = jnp.pad(w_gate_up[0, 1], ((GATE_RANK, LANES - 2 * GATE_RANK), (0, 0)))
    rpad = LANES - N_GROUPS - N_EXPERTS
    wr = jnp.pad(jnp.concatenate([w_router_group[0], w_router_expert[0]], axis=1), ((0, 0), (0, rpad)))
    br = jnp.pad(jnp.concatenate([b_router_group[0], b_router_expert[0]]), (0, rpad))[None]
    k = np.arange(FNET_GROUP_W)
    ang = 2.0 * math.pi * ((k[:, None] * k[None, :]) % FNET_GROUP_W) / FNET_GROUP_W
    return dict(
        nw1=norm1_w[0][None], wqk=w[:, :OFF_V].astype(BF16), wvg=w[:, OFF_V:OFF_ZF].astype(BF16),
        wz=wz.astype(BF16), wf=w[:, OFF_F:].astype(BF16),
        wg_f=_split_bf16(wg_f), wg_b=_split_bf16(wg_b), bg_f=b_gate_up[0, 0][None], bg_b=b_gate_up[0, 1][None],
        gw=gla_norm_w[0][None], cw=jnp.asarray(np.cos(ang), BF16), sw=jnp.asarray(np.sin(ang), BF16),
        wo=w_out[0].astype(BF16), nw2=norm2_w[0][None], wr=_split_bf16(wr), br=br,
        weg=w_expert_gate[0], weu=w_expert_up[0], wed=w_expert_down[0],
        fw=final_norm_w[None])


def _split_bf16(w):
    hi = w.astype(BF16)
    lo = (w - hi.astype(F32)).astype(BF16)
    return jnp.concatenate([hi, lo], axis=1)


def _mixer_and_route(x, meta, pw, base_in):
    b, s, d = x.shape
    length = s + N_META
    qk, vg, z, four = _inproj(x, meta["four"], pw["nw1"], pw["wqk"], pw["wvg"], pw["wz"], pw["wf"])
    s0 = _meta_state(meta["k"], meta["v"], meta["z"], pw["wg_f"], pw["bg_f"])
    o_f = _gla(qk, vg, z, pw["wg_f"], pw["bg_f"], s0, None, reverse=False)
    o = _gla(qk, vg, z, pw["wg_b"], pw["bg_b"], jnp.zeros_like(s0), o_f, reverse=True)
    vpk = _seqdft(four)
    return _outproj(o, vg, vpk, x, pw["gw"], pw["cw"], pw["sw"], pw["wo"], pw["nw2"], pw["wr"], pw["br"],
                    base_in, length)


def kernel(x_prompt, x_sample, meta_tokens, norm1_w, w_in, w_gate_up, b_gate_up, gla_norm_w, w_out, norm2_w,
           w_router_group, b_router_group, w_router_expert, b_router_expert, w_expert_gate, w_expert_up,
           w_expert_down, final_norm_w):
    pw = _prep_weights(norm1_w, w_in, w_gate_up, b_gate_up, gla_norm_w, w_out, norm2_w, w_router_group,
                       b_router_group, w_router_expert, b_router_expert, w_expert_gate, w_expert_up,
                       w_expert_down, final_norm_w)
    qk_m, vg_m, z_m, four_m = _inproj(meta_tokens[None], None, pw["nw1"], pw["wqk"], pw["wvg"], pw["wz"],
                                       pw["wf"])
    meta = dict(k=qk_m[0, :, GLA_KEY:], v=vg_m[0, :, :GLA_WIDTH], z=z_m[0], four=four_m)

    xs_in = (x_prompt, x_sample)
    base = jnp.zeros((1, LANES), F32)
    mixed = []
    for x in xs_in:
        h, hn, route, base = _mixer_and_route(x, meta, pw, base)
        mixed.append((h, hn, route))

    counts = base[0, :N_EXPERTS].astype(jnp.int32)
    padded = (counts + MOE_BM - 1) // MOE_BM * MOE_BM
    pends = jnp.cumsum(padded)
    pstart = (pends - padded).astype(F32)
    n_assign = sum(x.shape[0] * x.shape[1] for x in xs_in) * TOP_K
    nb = -(-(n_assign + N_EXPERTS * (MOE_BM - 1)) // MOE_BM)
    n_used = (pends[-1:] // MOE_BM).astype(jnp.int32)
    blk_row = jnp.arange(nb, dtype=jnp.int32) * MOE_BM
    block_expert = jnp.minimum(jnp.sum((pends[None, :] <= blk_row[:, None]).astype(jnp.int32), axis=1),
                               N_EXPERTS - 1)
    tail = n_used[0] + jnp.arange(N_EXPERTS, dtype=jnp.int32)
    zblk = jnp.concatenate([jnp.maximum(pends // MOE_BM - 1, 0), jnp.minimum(tail, nb - 1)]).astype(jnp.int32)
    zval = jnp.concatenate([padded > 0, tail < nb]).astype(jnp.int32)

    slot_list = []
    for (h, hn, route) in mixed:
        t = h.shape[0] * h.shape[1]
        r2 = route.reshape(t, LANES)
        e = r2[:, 0:2].astype(jnp.int32)
        onehot = (e[:, :, None] == jnp.arange(N_EXPERTS, dtype=jnp.int32)).astype(F32)
        slots = (jnp.sum(onehot * pstart, axis=-1) + r2[:, 4:6]).astype(jnp.int32).reshape(-1)
        slot_list.append(slots)
    xs = _dispatch(jnp.concatenate(slot_list), zblk, zval,
                   [hn.reshape(-1, D_MODEL) for (_, hn, _) in mixed], nb * MOE_BM)

    ys = _experts(block_expert, n_used, xs, pw["weg"], pw["weu"], pw["wed"])

    outs = []
    for x, (h, hn, route), slots in zip(xs_in, mixed, slot_list):
        t = h.shape[0] * h.shape[1]
        y = _combine(slots, h.reshape(t, D_MODEL), route.reshape(t, LANES), ys, pw["fw"])
        outs.append(y.reshape(x.shape))
    return tuple(outs)
```

```python
import functools
import math

import numpy as np
import jax
import jax.numpy as jnp
from jax import lax
from jax.experimental import pallas as pl
from jax.experimental.pallas import tpu as pltpu

F32 = jnp.float32
BF16 = jnp.bfloat16

D_MODEL = 2048
N_META = 16
GLA_WIDTH = 1024
FNET_WIDTH = 1024
GLA_HEADS = 4
HEAD_V = 256
GLA_KEY = 512
HEAD_K = 128
GATE_RANK = 16
GATE_TAU = 16.0
CHUNK = 64
FNET_GROUP_W = 256
N_GROUPS = 4
EXPERTS_PER_GROUP = 8
N_EXPERTS = 32
TOP_K = 2
EXPERT_FF = 512
EPS = 1e-6
OFF_K = 512
OFF_V = 1024
OFF_G = 2048
OFF_ZF = 3072
OFF_F = 3104

LANES = 128
MOE_BM = 512
VMEM_LIMIT = 56 * 1024 * 1024


def _cparams(semantics, vmem=VMEM_LIMIT):
    return pltpu.CompilerParams(dimension_semantics=semantics, vmem_limit_bytes=vmem)


def _resident(shape):
    nd = len(shape)
    return pl.BlockSpec(shape, lambda *_: (0,) * nd, pipeline_mode=pl.Buffered(1))


def _row_tile(n, target):
    t = min(n, target)
    while n % t:
        t -= 8
    return t


def _inproj_body(x_ref, fm_ref, nw_ref, wqk_ref, wvg_ref, wz_ref, wf_ref, qk_ref, vg_ref, z_ref, f_ref, *, nt):
    i = pl.program_id(1)

    @pl.when(i < nt)
    def _():
        x = x_ref[0]
        y = x * lax.rsqrt(jnp.mean(x * x, axis=-1, keepdims=True) + EPS)
        yb = (y * nw_ref[...]).astype(BF16)
        qk_ref[0] = jnp.dot(yb, wqk_ref[...], preferred_element_type=F32)
        vg_ref[0] = jnp.dot(yb, wvg_ref[...], preferred_element_type=F32)
        z_ref[0] = jnp.dot(yb, wz_ref[...], preferred_element_type=F32)
        f_ref[0] = jnp.dot(yb, wf_ref[...], preferred_element_type=F32)

    @pl.when(i == nt)
    def _():
        f_ref[0, 0:N_META, :] = fm_ref[0]


def _inproj(x, four_meta, nw, wqk, wvg, wz, wf):
    b, s, d = x.shape
    tm = _row_tile(s, 512)
    nt = s // tm
    extra = 0 if four_meta is None else 1
    if four_meta is None:
        four_meta = jnp.zeros((1, N_META, FNET_WIDTH), F32)
    row = lambda w: pl.BlockSpec((1, tm, w), lambda bi, i: (bi, jnp.minimum(i, nt - 1), 0))
    return pl.pallas_call(
        functools.partial(_inproj_body, nt=nt),
        grid=(b, nt + extra),
        in_specs=[row(d), _resident(four_meta.shape), _resident((1, d)), _resident(wqk.shape),
                  _resident(wvg.shape), _resident(wz.shape), _resident(wf.shape)],
        out_specs=[row(2 * GLA_KEY), row(2 * GLA_WIDTH), row(LANES),
                   pl.BlockSpec((1, tm, FNET_WIDTH), lambda bi, i: (bi, i, 0))],
        out_shape=[jax.ShapeDtypeStruct((b, s, 2 * GLA_KEY), F32),
                   jax.ShapeDtypeStruct((b, s, 2 * GLA_WIDTH), F32),
                   jax.ShapeDtypeStruct((b, s, LANES), F32),
                   jax.ShapeDtypeStruct((b, s + extra * N_META, FNET_WIDTH), F32)],
        compiler_params=_cparams(("parallel", "arbitrary")),
        name="inproj",
    )(x, four_meta, nw, wqk, wvg, wz, wf)


def _log_sigmoid(x):
    return jnp.minimum(x, 0.0) - jnp.log1p(jnp.exp(-jnp.abs(x)))


def _gate_log_decay(z, wg2, bg):
    n = z.shape[0]
    z_hi = z.astype(BF16)
    z_lo = (z - z_hi.astype(F32)).astype(BF16)
    pp = jnp.dot(jnp.concatenate([z_hi, z_lo], axis=0), wg2, preferred_element_type=F32)
    pre = ((pp[:n, :GLA_KEY] + pp[:n, GLA_KEY:]) + (pp[n:, :GLA_KEY] + pp[n:, GLA_KEY:])) + bg
    return _log_sigmoid(pre) * (1.0 / GATE_TAU)


def _chunk_cumsum(btri, la):
    la_hi = la.astype(BF16)
    la_lo = (la - la_hi.astype(F32)).astype(BF16)
    pp = jnp.dot(btri, jnp.concatenate([la_hi, la_lo], axis=1), preferred_element_type=F32)
    return pp[:, :GLA_KEY] + pp[:, GLA_KEY:]


def _block_tri(rows, reverse):
    r = np.arange(rows)
    same = (r[:, None] // CHUNK) == (r[None, :] // CHUNK)
    tri = (r[None, :] >= r[:, None]) if reverse else (r[None, :] <= r[:, None])
    return jnp.asarray(same & tri, BF16)


def _meta_state_body(k_ref, v_ref, z_ref, wg_ref, bg_ref, tri_ref, s_ref):
    la = _gate_log_decay(z_ref[...], wg_ref[...], bg_ref[...])
    bc = _chunk_cumsum(tri_ref[...], la)
    bl = bc[N_META - 1:N_META]
    ke = (k_ref[...] * jnp.exp(bl - bc)).astype(BF16)
    v = v_ref[...].astype(BF16)
    for h in range(GLA_HEADS):
        s_ref[h] = lax.dot_general(v[:, h * HEAD_V:(h + 1) * HEAD_V], ke[:, h * HEAD_K:(h + 1) * HEAD_K],
                                   (((0,), (0,)), ((), ())), preferred_element_type=F32)


def _meta_state(k_m, v_m, z_m, wg2, bg):
    return pl.pallas_call(
        _meta_state_body,
        out_shape=jax.ShapeDtypeStruct((GLA_HEADS, HEAD_V, HEAD_K), F32),
        name="gla_meta_state",
    )(k_m, v_m, z_m, wg2, bg, _block_tri(N_META, False))


def _gla_body(*refs, reverse, cb, has_prev):
    if has_prev:
        qk_ref, v_ref, z_ref, wg_ref, bg_ref, tri_ref, s0_ref, prev_ref, o_ref, st_ref = refs
    else:
        qk_ref, v_ref, z_ref, wg_ref, bg_ref, tri_ref, s0_ref, o_ref, st_ref = refs
        prev_ref = None

    @pl.when(pl.program_id(1) == 0)
    def _():
        st_ref[...] = s0_ref[...]

    nrow = cb * CHUNK
    row = lax.broadcasted_iota(jnp.int32, (nrow, nrow), 0)
    col = lax.broadcasted_iota(jnp.int32, (nrow, nrow), 1)
    shift = CHUNK.bit_length() - 1
    same_chunk = (row >> shift) == (col >> shift)
    msk = jnp.logical_and(same_chunk, (col > row) if reverse else (col <= row))
    scale = HEAD_K ** -0.5
    contract_last = (((1,), (1,)), ((), ()))
    contract_first = (((0,), (0,)), ((), ()))
    la = _gate_log_decay(z_ref[0], wg_ref[...], bg_ref[...])
    bc = _chunk_cumsum(tri_ref[...], la)
    chunks = [slice(c * CHUNK, (c + 1) * CHUNK) for c in range(cb)]
    edge = [bc[c * CHUNK:c * CHUNK + 1] if reverse else bc[(c + 1) * CHUNK - 1:(c + 1) * CHUNK]
            for c in range(cb)]
    bl = jnp.concatenate([jnp.broadcast_to(e, (CHUNK, GLA_KEY)) for e in edge], axis=0)
    q = qk_ref[0, :, 0:GLA_KEY] * scale
    k = qk_ref[0, :, GLA_KEY:2 * GLA_KEY]
    qd = (q * jnp.exp(bc)).astype(BF16)
    kd = (k * jnp.exp(-bc)).astype(BF16)
    ke = (k * jnp.exp(bl - bc)).astype(BF16)
    dec = [jnp.exp(e) for e in edge]
    v = v_ref[0].astype(BF16)
    order = range(cb - 1, -1, -1) if reverse else range(cb)
    for h in range(GLA_HEADS):
        ks = slice(h * HEAD_K, (h + 1) * HEAD_K)
        vs = slice(h * HEAD_V, (h + 1) * HEAD_V)
        sc = lax.dot_general(qd[:, ks], kd[:, ks], contract_last, preferred_element_type=F32)
        sc = jnp.where(msk, sc, 0.0).astype(BF16)
        o_intra = jnp.dot(sc, v[:, vs], preferred_element_type=F32)
        st = st_ref[h]
        for c in order:
            rows = chunks[c]
            o = o_intra[rows] + lax.dot_general(qd[rows, ks], st.astype(BF16), contract_last,
                                                preferred_element_type=F32)
            upd = lax.dot_general(v[rows, vs], ke[rows, ks], contract_first, preferred_element_type=F32)
            st = st * dec[c][:, ks] + upd
            if prev_ref is not None:
                o = prev_ref[0, rows, vs] + o
            o_ref[0, rows, vs] = o
        st_ref[h] = st


def _gla(qk, vg, z, wg2, bg, s0, prev, reverse):
    b, s, _ = qk.shape
    cb = 4
    rows = cb * CHUNK
    nb = s // rows
    btri = _block_tri(rows, reverse)
    if reverse:
        blk = lambda bi, i: (bi, nb - 1 - i, 0)
    else:
        blk = lambda bi, i: (bi, i, 0)
    in_specs = [pl.BlockSpec((1, rows, 2 * GLA_KEY), blk),
                pl.BlockSpec((1, rows, GLA_WIDTH), blk),
                pl.BlockSpec((1, rows, LANES), blk),
                _resident(wg2.shape), _resident(bg.shape), _resident(btri.shape), _resident(s0.shape)]
    args = [qk, vg, z, wg2, bg, btri, s0]
    if prev is not None:
        in_specs.append(pl.BlockSpec((1, rows, GLA_WIDTH), blk))
        args.append(prev)
    return pl.pallas_call(
        functools.partial(_gla_body, reverse=reverse, cb=cb, has_prev=prev is not None),
        grid=(b, nb),
        in_specs=in_specs,
        out_specs=pl.BlockSpec((1, rows, GLA_WIDTH), blk),
        out_shape=jax.ShapeDtypeStruct((b, s, GLA_WIDTH), F32),
        scratch_shapes=[pltpu.VMEM((GLA_HEADS, HEAD_V, HEAD_K), F32)],
        compiler_params=_cparams(("parallel", "arbitrary")),
        name="gla_bwd" if reverse else "gla_fwd",
    )(*args)


def _dft_factors(length):
    best = None
    for n2 in range(8, length, 8):
        if length % n2 == 0:
            n1 = length // n2
            if best is None or n2 + 2 * n1 < best[1] + 2 * best[0]:
                best = (n1, n2)
    assert best is not None, length
    return best


def _dft_plan(length, channels):
    n1, n2 = _dft_factors(length)
    kc = _round_up(n1, 16)
    pitch = _round_up(n1, 8)
    if (pitch // 8) % 2 == 0:
        pitch += 8
    rows = _round_up(n2 * pitch + max(0, kc - pitch), 8)
    cw = LANES
    assert channels % cw == 0
    group = 2
    return n1, n2, kc, pitch, rows, cw, group


def _round_up(x, m):
    return (x + m - 1) // m * m


def _dft_tables(length, n1, n2, n1p):
    two_pi = 2.0 * math.pi
    r = np.arange(n2)
    pa = ((r[:, None] + N_META) * r[None, :]) % n2
    ang = two_pi * pa / n2
    a_stack = np.concatenate([np.cos(ang), -np.sin(ang)], axis=0)
    r1 = np.arange(n1)
    pb = (r1[:, None] * (r1[None, :] + N_META)) % n1
    angb = two_pi * pb / n1
    br = np.zeros((n1p, n1p)); bi = np.zeros((n1p, n1p))
    br[:n1, :n1] = np.cos(angb); bi[:n1, :n1] = -np.sin(angb)
    pt = ((r[:, None] + N_META) * (r1[None, :] + N_META)) % length
    angt = two_pi * pt / length
    twr = np.zeros((n2, 1, n1p)); twi = np.zeros((n2, 1, n1p))
    twr[:, 0, :n1] = np.cos(angt); twi[:, 0, :n1] = -np.sin(angt)
    return (jnp.asarray(a_stack, BF16), jnp.asarray(br, F32), jnp.asarray(bi, F32),
            jnp.asarray(twr, F32), jnp.asarray(twi, F32))


def _bf16_bits(x):
    return pltpu.bitcast(x.astype(BF16).astype(F32), jnp.uint32)


def _seqdft_body(x_ref, a_ref, br_ref, bi_ref, twr_ref, twi_ref, o_ref, ur_ref, ui_ref, *,
                 n1, n2, kc, pitch, cw, group):
    @pl.when(jnp.logical_and(pl.program_id(0) == 0, pl.program_id(1) == 0))
    def _():
        ur_ref[...] = jnp.zeros(ur_ref.shape, F32)
        ui_ref[...] = jnp.zeros(ui_ref.shape, F32)

    a = a_ref[...]

    def slab_group(m1, cnt):
        slabs = [x_ref[0, pl.ds(m1 + t, n2, stride=n1), :] for t in range(cnt)]
        rhs = (slabs[0] if cnt == 1 else jnp.concatenate(slabs, axis=1)).astype(BF16)
        p = jnp.dot(a, rhs, preferred_element_type=F32)
        for t in range(cnt):
            ur_ref[pl.ds(m1 + t, n2, stride=pitch), :] = p[:n2, t * cw:(t + 1) * cw]
            ui_ref[pl.ds(m1 + t, n2, stride=pitch), :] = p[n2:, t * cw:(t + 1) * cw]

    def step_a(it, carry):
        slab_group(it * group, group)
        return carry

    lax.fori_loop(0, n1 // group, step_a, 0, unroll=4)
    if n1 % group:
        slab_group((n1 // group) * group, n1 % group)

    br = br_ref[...]
    bi = bi_ref[...]

    def step_c(j, carry):
        tr = twr_ref[j]
        ti = twi_ref[j]
        fm = jnp.concatenate([br * tr - bi * ti, br * ti + bi * tr], axis=0).astype(BF16)
        start = pl.multiple_of(j * pitch, 8)
        rhs = jnp.concatenate([ur_ref[pl.ds(start, kc), :], ui_ref[pl.ds(start, kc), :]],
                              axis=1).astype(BF16)
        r = jnp.dot(fm, rhs, preferred_element_type=F32)
        vr = r[:n1, :cw] - r[kc:kc + n1, cw:]
        vi = r[:n1, cw:] + r[kc:kc + n1, :cw]
        o_ref[0, pl.ds(j, n1, stride=n2), :] = _bf16_bits(vr) | (_bf16_bits(vi) >> 16)
        return carry

    lax.fori_loop(0, n2, step_c, 0, unroll=4)


def _seqdft(four):
    b, length, c = four.shape
    n1, n2, kc, pitch, rows, cw, group = _dft_plan(length, c)
    tables = _dft_tables(length, n1, n2, kc)
    return pl.pallas_call(
        functools.partial(_seqdft_body, n1=n1, n2=n2, kc=kc, pitch=pitch, cw=cw, group=group),
        grid=(b, c // cw),
        in_specs=[pl.BlockSpec((1, length, cw), lambda bi, ci: (bi, 0, ci), pipeline_mode=pl.Buffered(1))]
                 + [_resident(t.shape) for t in tables],
        out_specs=pl.BlockSpec((1, length, cw), lambda bi, ci: (bi, 0, ci)),
        out_shape=jax.ShapeDtypeStruct((b, length, c), jnp.uint32),
        scratch_shapes=[pltpu.VMEM((rows, cw), F32), pltpu.VMEM((rows, cw), F32)],
        compiler_params=_cparams(("arbitrary", "arbitrary")),
        name="seqdft",
    )(four, *tables)


def _outproj_body(o_ref, g_ref, v_ref, x_ref, gw_ref, cw_ref, sw_ref, wo_ref, n2w_ref, wr_ref, br_ref,
                  base_in_ref, h_ref, hn_ref, route_ref, cnt_ref, base_ref, *, fnorm):
    first = jnp.logical_and(pl.program_id(0) == 0, pl.program_id(1) == 0)

    @pl.when(first)
    def _():
        base_ref[...] = base_in_ref[...]

    tm = o_ref.shape[1]
    parts = []
    for h in range(GLA_HEADS):
        vs = slice(h * HEAD_V, (h + 1) * HEAD_V)
        o = o_ref[0, :, vs]
        o = o * lax.rsqrt(jnp.mean(o * o, axis=-1, keepdims=True) + EPS)
        o = o * gw_ref[:, vs]
        g = g_ref[0, :, vs]
        parts.append((o * (g * (1.0 / (1.0 + jnp.exp(-g))))).astype(BF16))
    cw = cw_ref[...]
    sw = sw_ref[...]
    for gi in range(FNET_WIDTH // FNET_GROUP_W):
        cs = slice(gi * FNET_GROUP_W, (gi + 1) * FNET_GROUP_W)
        w = v_ref[0, :, cs]
        vr = pltpu.bitcast(w & jnp.uint32(0xFFFF0000), F32).astype(BF16)
        vi = pltpu.bitcast(w << 16, F32).astype(BF16)
        f = (jnp.dot(vr, cw, preferred_element_type=F32) + jnp.dot(vi, sw, preferred_element_type=F32))
        parts.append((f * fnorm).astype(BF16))
    merged = jnp.concatenate(parts, axis=-1)
    h1 = x_ref[0] + jnp.dot(merged, wo_ref[...], preferred_element_type=F32)
    h_ref[0] = h1
    hn = h1 * lax.rsqrt(jnp.mean(h1 * h1, axis=-1, keepdims=True) + EPS) * n2w_ref[...]
    hn_ref[0] = hn

    hn_hi = hn.astype(BF16)
    hn_lo = (hn - hn_hi.astype(F32)).astype(BF16)
    pp = jnp.dot(jnp.concatenate([hn_hi, hn_lo], axis=0), wr_ref[...], preferred_element_type=F32)
    logits = ((pp[:tm, :LANES] + pp[:tm, LANES:]) + (pp[tm:, :LANES] + pp[tm:, LANES:])) + br_ref[...]
    lane = lax.broadcasted_iota(jnp.int32, (tm, LANES), 1)
    neg = jnp.float32(-jnp.inf)
    gl = jnp.where(lane < N_GROUPS, logits, neg)
    gmax = jnp.max(gl, axis=-1, keepdims=True)
    grp = jnp.min(jnp.where(gl == gmax, lane, LANES), axis=-1, keepdims=True)
    grp_w = 1.0 / jnp.sum(jnp.exp(gl - gmax), axis=-1, keepdims=True)
    elane = lane - N_GROUPS
    member = jnp.logical_and(jnp.logical_and(elane >= 0, elane < N_EXPERTS),
                             (elane >> 3) == grp)
    el = jnp.where(member, logits, neg)
    m0 = jnp.max(el, axis=-1, keepdims=True)
    i0 = jnp.min(jnp.where(el == m0, lane, LANES), axis=-1, keepdims=True)
    el2 = jnp.where(lane == i0, neg, el)
    m1 = jnp.max(el2, axis=-1, keepdims=True)
    i1 = jnp.min(jnp.where(el2 == m1, lane, LANES), axis=-1, keepdims=True)
    p1 = jnp.exp(m1 - m0)
    den = 1.0 + p1
    w0 = (1.0 / den) * grp_w
    w1 = (p1 / den) * grp_w
    e0 = i0 - N_GROUPS
    e1 = i1 - N_GROUPS

    oh0 = (lane == e0).astype(BF16)
    oh1 = (lane == e1).astype(BF16)
    r_i = lax.broadcasted_iota(jnp.int32, (tm, tm), 0)
    c_i = lax.broadcasted_iota(jnp.int32, (tm, tm), 1)
    ltri = (c_i < r_i).astype(BF16)
    c0 = jnp.dot(ltri, oh0, preferred_element_type=F32)
    c1 = jnp.dot(ltri, oh1, preferred_element_type=F32)
    oh0f = oh0.astype(F32)
    oh1f = oh1.astype(F32)
    tot0 = jnp.sum(oh0f, axis=0, keepdims=True)
    tot1 = jnp.sum(oh1f, axis=0, keepdims=True)
    base = base_ref[...]
    rank0 = jnp.sum(oh0f * (c0 + base), axis=-1, keepdims=True)
    rank1 = jnp.sum(oh1f * (c1 + base + tot0), axis=-1, keepdims=True)
    new_base = base + tot0 + tot1
    base_ref[...] = new_base
    cnt_ref[...] = new_base

    route = jnp.where(lane == 0, e0.astype(F32),
            jnp.where(lane == 1, e1.astype(F32),
            jnp.where(lane == 2, w0,
            jnp.where(lane == 3, w1,
            jnp.where(lane == 4, rank0,
            jnp.where(lane == 5, rank1, 0.0))))))
    route_ref[0] = route


def _outproj(o, vg, vpk, x, gw, cw, sw, wo, n2w, wr, br, base_in, length):
    b, s, d = x.shape
    tm = _row_tile(s, 512)
    fnorm = 1.0 / math.sqrt(length * FNET_GROUP_W)
    row = lambda w, col=0: pl.BlockSpec((1, tm, w), lambda bi, i: (bi, i, col))
    return pl.pallas_call(
        functools.partial(_outproj_body, fnorm=fnorm),
        grid=(b, s // tm),
        in_specs=[row(GLA_WIDTH), row(GLA_WIDTH, 1), row(FNET_WIDTH), row(d),
                  _resident(gw.shape), _resident(cw.shape), _resident(sw.shape), _resident(wo.shape),
                  _resident(n2w.shape), _resident(wr.shape), _resident(br.shape), _resident(base_in.shape)],
        out_specs=[row(d), row(d), row(LANES), pl.BlockSpec((1, LANES), lambda bi, i: (0, 0))],
        out_shape=[jax.ShapeDtypeStruct((b, s, d), F32), jax.ShapeDtypeStruct((b, s, d), F32),
                   jax.ShapeDtypeStruct((b, s, LANES), F32), jax.ShapeDtypeStruct((1, LANES), F32)],
        scratch_shapes=[pltpu.VMEM((1, LANES), F32)],
        compiler_params=_cparams(("arbitrary", "arbitrary")),
        name="outproj_router",
    )(o, vg, vpk, x, gw, cw, sw, wo, n2w, wr, br, base_in)


DISPATCH_WINDOW = 256


def _dispatch_body(slot_ref, zblk_ref, zval_ref, *refs, counts, win):
    srcs = refs[:len(counts)]
    xs_out, zero_ref, tile_ref, zsem, lsems, ssems = refs[len(counts):]
    step = pl.program_id(0)
    nsteps = sum(counts) // win

    @pl.when(step == 0)
    def _():
        zero_ref[...] = jnp.zeros(zero_ref.shape, F32)

        def zcopy(i):
            row = pl.multiple_of(zblk_ref[i] * MOE_BM, MOE_BM)
            return pltpu.make_async_copy(zero_ref, xs_out.at[pl.ds(row, MOE_BM)], zsem)

        for i in range(2 * N_EXPERTS):
            @pl.when(zval_ref[i] != 0)
            def _():
                zcopy(i).start()
        for i in range(2 * N_EXPERTS):
            @pl.when(zval_ref[i] != 0)
            def _():
                zcopy(i).wait()

    def load(w, start):
        lo = 0
        for src, n in zip(srcs, counts):
            hi = lo + n // win

            @pl.when(jnp.logical_and(w >= lo, w < hi))
            def _(src=src, lo=lo):
                row0 = pl.multiple_of((w - lo) * win, 8)
                b = lax.rem(w, 3)
                cp = pltpu.make_async_copy(src.at[pl.ds(row0, win)], tile_ref.at[b], lsems.at[b])
                if start:
                    cp.start()
                else:
                    cp.wait()

            lo = hi

    def scatter(w, start):
        b = lax.rem(w, 3)
        sem = ssems.at[lax.rem(w, 2)]
        tok0 = w * win

        def body(r, c):
            for k in range(TOP_K):
                s = slot_ref[2 * (tok0 + r) + k]
                cp = pltpu.make_async_copy(tile_ref.at[b, pl.ds(r, 1)], xs_out.at[pl.ds(s, 1)], sem)
                if start:
                    cp.start()
                else:
                    cp.wait()
            return c

        lax.fori_loop(0, win, body, 0, unroll=8)

    @pl.when(step == 0)
    def _():
        load(step, True)

    @pl.when(step + 1 < nsteps)
    def _():
        load(step + 1, True)

    load(step, False)
    scatter(step, True)

    @pl.when(step > 0)
    def _():
        scatter(step - 1, False)

    @pl.when(step == nsteps - 1)
    def _():
        scatter(step, False)


def _dispatch(slots, zblk, zval, sources, n_rows):
    d = sources[0].shape[1]
    counts = tuple(s.shape[0] for s in sources)
    win = functools.reduce(math.gcd, counts + (DISPATCH_WINDOW,))
    return pl.pallas_call(
        functools.partial(_dispatch_body, counts=counts, win=win),
        grid_spec=pltpu.PrefetchScalarGridSpec(
            num_scalar_prefetch=3,
            grid=(sum(counts) // win,),
            in_specs=[pl.BlockSpec(memory_space=pl.ANY) for _ in sources],
            out_specs=pl.BlockSpec(memory_space=pl.ANY),
            scratch_shapes=[pltpu.VMEM((MOE_BM, d), F32), pltpu.VMEM((3, win, d), F32),
                            pltpu.SemaphoreType.DMA(()), pltpu.SemaphoreType.DMA((3,)),
                            pltpu.SemaphoreType.DMA((2,))]),
        out_shape=jax.ShapeDtypeStruct((n_rows, d), F32),
        compiler_params=_cparams(("arbitrary",)),
        name="moe_dispatch",
    )(slots, zblk, zval, *sources)


def _experts_body(be_ref, nu_ref, x_ref, wg_ref, wu_ref, wd_ref, y_ref, wgb_ref, wub_ref, wdb_ref):
    i = pl.program_id(0)
    used = i < nu_ref[0]
    ic = jnp.minimum(i, nu_ref[0] - 1)
    fresh = jnp.logical_or(i == 0, be_ref[ic] != be_ref[jnp.maximum(ic - 1, 0)])

    @pl.when(jnp.logical_and(used, fresh))
    def _():
        wgb_ref[...] = wg_ref[0].astype(BF16)
        wub_ref[...] = wu_ref[0].astype(BF16)
        wdb_ref[...] = wd_ref[0].astype(BF16)

    @pl.when(used)
    def _():
        xb = x_ref[...].astype(BF16)
        gate = jnp.dot(xb, wgb_ref[...], preferred_element_type=F32)
        up = jnp.dot(xb, wub_ref[...], preferred_element_type=F32)
        hid = (gate * (1.0 / (1.0 + jnp.exp(-gate))) * up).astype(BF16)
        y_ref[...] = jnp.dot(hid, wdb_ref[...], preferred_element_type=F32)

    @pl.when(jnp.logical_not(used))
    def _():
        y_ref[...] = jnp.zeros(y_ref.shape, F32)


def _experts(block_expert, n_used, xs, wg, wu, wd):
    p, d = xs.shape
    nb = p // MOE_BM
    blk = lambda i, be, nu: (jnp.minimum(i, nu[0] - 1), 0)
    wsel = lambda i, be, nu: (be[jnp.minimum(i, nu[0] - 1)], 0, 0)
    return pl.pallas_call(
        _experts_body,
        grid_spec=pltpu.PrefetchScalarGridSpec(
            num_scalar_prefetch=2,
            grid=(nb,),
            in_specs=[pl.BlockSpec((MOE_BM, d), blk),
                      pl.BlockSpec((1, d, EXPERT_FF), wsel),
                      pl.BlockSpec((1, d, EXPERT_FF), wsel),
                      pl.BlockSpec((1, EXPERT_FF, d), wsel)],
            out_specs=pl.BlockSpec((MOE_BM, d), lambda i, be, nu: (i, 0)),
            scratch_shapes=[pltpu.VMEM((d, EXPERT_FF), BF16), pltpu.VMEM((d, EXPERT_FF), BF16),
                            pltpu.VMEM((EXPERT_FF, d), BF16)]),
        out_shape=jax.ShapeDtypeStruct((p, d), F32),
        compiler_params=_cparams(("arbitrary",)),
        name="moe_experts",
    )(block_expert, n_used, xs, wg, wu, wd)


def _combine_body(slot_ref, h_ref, route_ref, ys_any, fw_ref, o_ref, ybuf_ref, sems, *, nt):
    tm = h_ref.shape[0]
    i = pl.program_id(0)

    def copy(tile, r, k):
        buf = lax.rem(tile, 2)
        s = slot_ref[2 * (tile * tm + r) + k]
        return pltpu.make_async_copy(ys_any.at[pl.ds(s, 1)], ybuf_ref.at[buf, k, pl.ds(r, 1)], sems.at[buf])

    def issue_tile(tile):
        def body(r, carry):
            copy(tile, r, 0).start()
            copy(tile, r, 1).start()
            return carry
        lax.fori_loop(0, tm, body, 0, unroll=8)

    @pl.when(i == 0)
    def _():
        issue_tile(i)

    @pl.when(i + 1 < nt)
    def _():
        issue_tile(i + 1)

    def drain(r, carry):
        copy(i, r, 0).wait()
        copy(i, r, 1).wait()
        return carry

    lax.fori_loop(0, tm, drain, 0, unroll=8)

    buf = lax.rem(i, 2)
    w0 = route_ref[:, 2:3]
    w1 = route_ref[:, 3:4]
    h = h_ref[...] + (ybuf_ref[buf, 0] * w0 + ybuf_ref[buf, 1] * w1)
    o_ref[...] = h * lax.rsqrt(jnp.mean(h * h, axis=-1, keepdims=True) + EPS) * fw_ref[...]


def _combine(slots, h2d, route2d, ys, fw):
    t, d = h2d.shape
    tm = _row_tile(t, 256)
    nt = t // tm
    return pl.pallas_call(
        functools.partial(_combine_body, nt=nt),
        grid_spec=pltpu.PrefetchScalarGridSpec(
            num_scalar_prefetch=1,
            grid=(nt,),
            in_specs=[pl.BlockSpec((tm, d), lambda i, sl: (i, 0)),
                      pl.BlockSpec((tm, LANES), lambda i, sl: (i, 0)),
                      pl.BlockSpec(memory_space=pl.ANY),
                      pl.BlockSpec((1, d), lambda i, sl: (0, 0))],
            out_specs=pl.BlockSpec((tm, d), lambda i, sl: (i, 0)),
            scratch_shapes=[pltpu.VMEM((2, TOP_K, tm, d), F32), pltpu.SemaphoreType.DMA((2,))]),
        out_shape=jax.ShapeDtypeStruct((t, d), F32),
        compiler_params=_cparams(("arbitrary",)),
        name="moe_combine",
    )(slots, h2d, route2d, ys, fw)


def _prep_weights(norm1_w, w_in, w_gate_up, b_gate_up, gla_norm_w, w_out, norm2_w, w_router_group,
                  b_router_group, w_router_expert, b_router_expert, w_expert_gate, w_expert_up,
                  w_expert_down, final_norm_w):
    w = w_in[0]
    wz = jnp.pad(w[:, OFF_ZF:OFF_F], ((0, 0), (0, LANES - 2 * GATE_RANK)))
    wg_f = jnp.pad(w_gate_up[0, 0], ((0, LANES - GATE_RANK), (0, 0)))
    wg_b = jnp.pad(w_gate_up[0, 1], ((GATE_RANK, LANES - 2 * GATE_RANK), (0, 0)))
    rpad = LANES - N_GROUPS - N_EXPERTS
    wr = jnp.pad(jnp.concatenate([w_router_group[0], w_router_expert[0]], axis=1), ((0, 0), (0, rpad)))
    br = jnp.pad(jnp.concatenate([b_router_group[0], b_router_expert[0]]), (0, rpad))[None]
    k = np.arange(FNET_GROUP_W)
    ang = 2.0 * math.pi * ((k[:, None] * k[None, :]) % FNET_GROUP_W) / FNET_GROUP_W
    return dict(
        nw1=norm1_w[0][None], wqk=w[:, :OFF_V].astype(BF16), wvg=w[:, OFF_V:OFF_ZF].astype(BF16),
        wz=wz.astype(BF16), wf=w[:, OFF_F:].astype(BF16),
        wg_f=_split_bf16(wg_f), wg_b=_split_bf16(wg_b), bg_f=b_gate_up[0, 0][None], bg_b=b_gate_up[0, 1][None],
        gw=gla_norm_w[0][None], cw=jnp.asarray(np.cos(ang), BF16), sw=jnp.asarray(np.sin(ang), BF16),
        wo=w_out[0].astype(BF16), nw2=norm2_w[0][None], wr=_split_bf16(wr), br=br,
        weg=w_expert_gate[0], weu=w_expert_up[0], wed=w_expert_down[0],
        fw=final_norm_w[None])


def _split_bf16(w):
    hi = w.astype(BF16)
    lo = (w - hi.astype(F32)).astype(BF16)
    return jnp.concatenate([hi, lo], axis=1)


def _mixer_and_route(x, meta, pw, base_in):
    b, s, d = x.shape
    length = s + N_META
    qk, vg, z, four = _inproj(x, meta["four"], pw["nw1"], pw["wqk"], pw["wvg"], pw["wz"], pw["wf"])
    s0 = _meta_state(meta["k"], meta["v"], meta["z"], pw["wg_f"], pw["bg_f"])
    o_f = _gla(qk, vg, z, pw["wg_f"], pw["bg_f"], s0, None, reverse=False)
    o = _gla(qk, vg, z, pw["wg_b"], pw["bg_b"], jnp.zeros_like(s0), o_f, reverse=True)
    vpk = _seqdft(four)
    return _outproj(o, vg, vpk, x, pw["gw"], pw["cw"], pw["sw"], pw["wo"], pw["nw2"], pw["wr"], pw["br"],
                    base_in, length)


def kernel(x_prompt, x_sample, meta_tokens, norm1_w, w_in, w_gate_up, b_gate_up, gla_norm_w, w_out, norm2_w,
           w_router_group, b_router_group, w_router_expert, b_router_expert, w_expert_gate, w_expert_up,
           w_expert_down, final_norm_w):
    pw = _prep_weights(norm1_w, w_in, w_gate_up, b_gate_up, gla_norm_w, w_out, norm2_w, w_router_group,
                       b_router_group, w_router_expert, b_router_expert, w_expert_gate, w_expert_up,
                       w_expert_down, final_norm_w)
    qk_m, vg_m, z_m, four_m = _inproj(meta_tokens[None], None, pw["nw1"], pw["wqk"], pw["wvg"], pw["wz"],
                                       pw["wf"])
    meta = dict(k=qk_m[0, :, GLA_KEY:], v=vg_m[0, :, :GLA_WIDTH], z=z_m[0], four=four_m)

    xs_in = (x_prompt, x_sample)
    base = jnp.zeros((1, LANES), F32)
    mixed = []
    for x in xs_in:
        h, hn, route, base = _mixer_and_route(x, meta, pw, base)
        mixed.append((h, hn, route))

    counts = base[0, :N_EXPERTS].astype(jnp.int32)
    padded = (counts + MOE_BM - 1) // MOE_BM * MOE_BM
    pends = jnp.cumsum(padded)
    pstart = (pends - padded).astype(F32)
    n_assign = sum(x.shape[0] * x.shape[1] for x in xs_in) * TOP_K
    nb = -(-(n_assign + N_EXPERTS * (MOE_BM - 1)) // MOE_BM)
    n_used = (pends[-1:] // MOE_BM).astype(jnp.int32)
    blk_row = jnp.arange(nb, dtype=jnp.int32) * MOE_BM
    block_expert = jnp.minimum(jnp.sum((pends[None, :] <= blk_row[:, None]).astype(jnp.int32), axis=1),
                               N_EXPERTS - 1)
    tail = n_used[0] + jnp.arange(N_EXPERTS, dtype=jnp.int32)
    zblk = jnp.concatenate([jnp.maximum(pends // MOE_BM - 1, 0), jnp.minimum(tail, nb - 1)]).astype(jnp.int32)
    zval = jnp.concatenate([padded > 0, tail < nb]).astype(jnp.int32)

    slot_list = []
    for (h, hn, route) in mixed:
        t = h.shape[0] * h.shape[1]
        r2 = route.reshape(t, LANES)
        e = r2[:, 0:2].astype(jnp.int32)
        onehot = (e[:, :, None] == jnp.arange(N_EXPERTS, dtype=jnp.int32)).astype(F32)
        slots = (jnp.sum(onehot * pstart, axis=-1) + r2[:, 4:6]).astype(jnp.int32).reshape(-1)
        slot_list.append(slots)
    xs = _dispatch(jnp.concatenate(slot_list), zblk, zval,
                   [hn.reshape(-1, D_MODEL) for (_, hn, _) in mixed], nb * MOE_BM)

    ys = _experts(block_expert, n_used, xs, pw["weg"], pw["weu"], pw["wed"])

    outs = []
    for x, (h, hn, route), slots in zip(xs_in, mixed, slot_list):
        t = h.shape[0] * h.shape[1]
        y = _combine(slots, h.reshape(t, D_MODEL), route.reshape(t, LANES), ys, pw["fw"])
        outs.append(y.reshape(x.shape))
    return tuple(outs)
```

```python
import functools
import math

import numpy as np
import jax
import jax.numpy as jnp
from jax import lax
from jax.experimental import pallas as pl
from jax.experimental.pallas import tpu as pltpu

F32 = jnp.float32
BF16 = jnp.bfloat16

D_MODEL = 2048
N_META = 16
GLA_WIDTH = 1024
FNET_WIDTH = 1024
GLA_HEADS = 4
HEAD_V = 256
GLA_KEY = 512
HEAD_K = 128
GATE_RANK = 16
GATE_TAU = 16.0
CHUNK = 64
FNET_GROUP_W = 256
N_GROUPS = 4
EXPERTS_PER_GROUP = 8
N_EXPERTS = 32
TOP_K = 2
EXPERT_FF = 512
EPS = 1e-6
OFF_K = 512
OFF_V = 1024
OFF_G = 2048
OFF_ZF = 3072
OFF_F = 3104

LANES = 128
MOE_BM = 512
VMEM_LIMIT = 56 * 1024 * 1024


def _cparams(semantics, vmem=VMEM_LIMIT):
    return pltpu.CompilerParams(dimension_semantics=semantics, vmem_limit_bytes=vmem)


def _resident(shape):
    nd = len(shape)
    return pl.BlockSpec(shape, lambda *_: (0,) * nd, pipeline_mode=pl.Buffered(1))


def _row_tile(n, target):
    t = min(n, target)
    while n % t:
        t -= 8
    return t


def _inproj_body(x_ref, fm_ref, nw_ref, wqk_ref, wv_ref, wg_ref, wz_ref, wf_ref,
                 qk_ref, v_ref, g_ref, z_ref, f_ref, *, nt):
    i = pl.program_id(1)

    @pl.when(i < nt)
    def _():
        x = x_ref[0]
        y = x * lax.rsqrt(jnp.mean(x * x, axis=-1, keepdims=True) + EPS)
        yb = (y * nw_ref[...]).astype(BF16)
        qk_ref[0] = jnp.dot(yb, wqk_ref[...], preferred_element_type=F32)
        v_ref[0] = jnp.dot(yb, wv_ref[...], preferred_element_type=F32).astype(BF16)
        g_ref[0] = jnp.dot(yb, wg_ref[...], preferred_element_type=F32)
        z_ref[0] = jnp.dot(yb, wz_ref[...], preferred_element_type=F32)
        f_ref[0] = jnp.dot(yb, wf_ref[...], preferred_element_type=F32)

    @pl.when(i == nt)
    def _():
        f_ref[0, 0:N_META, :] = fm_ref[0]


def _inproj(x, four_meta, nw, wqk, wv, wg, wz, wf):
    b, s, d = x.shape
    tm = _row_tile(s, 512)
    nt = s // tm
    extra = 0 if four_meta is None else 1
    if four_meta is None:
        four_meta = jnp.zeros((1, N_META, FNET_WIDTH), F32)
    row = lambda w: pl.BlockSpec((1, tm, w), lambda bi, i: (bi, jnp.minimum(i, nt - 1), 0))
    return pl.pallas_call(
        functools.partial(_inproj_body, nt=nt),
        grid=(b, nt + extra),
        in_specs=[row(d), _resident(four_meta.shape), _resident((1, d)), _resident(wqk.shape),
                  _resident(wv.shape), _resident(wg.shape), _resident(wz.shape), _resident(wf.shape)],
        out_specs=[row(2 * GLA_KEY), row(GLA_WIDTH), row(GLA_WIDTH), row(LANES),
                   pl.BlockSpec((1, tm, FNET_WIDTH), lambda bi, i: (bi, i, 0))],
        out_shape=[jax.ShapeDtypeStruct((b, s, 2 * GLA_KEY), F32),
                   jax.ShapeDtypeStruct((b, s, GLA_WIDTH), BF16),
                   jax.ShapeDtypeStruct((b, s, GLA_WIDTH), F32),
                   jax.ShapeDtypeStruct((b, s, LANES), F32),
                   jax.ShapeDtypeStruct((b, s + extra * N_META, FNET_WIDTH), F32)],
        compiler_params=_cparams(("parallel", "arbitrary")),
        name="inproj",
    )(x, four_meta, nw, wqk, wv, wg, wz, wf)


def _log_sigmoid(x):
    return jnp.minimum(x, 0.0) - jnp.log1p(jnp.exp(-jnp.abs(x)))


def _gate_log_decay(z, wg2, bg):
    n = z.shape[0]
    z_hi = z.astype(BF16)
    z_lo = (z - z_hi.astype(F32)).astype(BF16)
    pp = jnp.dot(jnp.concatenate([z_hi, z_lo], axis=0), wg2, preferred_element_type=F32)
    pre = ((pp[:n, :GLA_KEY] + pp[:n, GLA_KEY:]) + (pp[n:, :GLA_KEY] + pp[n:, GLA_KEY:])) + bg
    return _log_sigmoid(pre) * (1.0 / GATE_TAU)


def _chunk_cumsum(btri, la):
    la_hi = la.astype(BF16)
    la_lo = (la - la_hi.astype(F32)).astype(BF16)
    pp = jnp.dot(btri, jnp.concatenate([la_hi, la_lo], axis=1), preferred_element_type=F32)
    return pp[:, :GLA_KEY] + pp[:, GLA_KEY:]


def _block_tri(rows, reverse):
    r = np.arange(rows)
    same = (r[:, None] // CHUNK) == (r[None, :] // CHUNK)
    tri = (r[None, :] >= r[:, None]) if reverse else (r[None, :] <= r[:, None])
    return jnp.asarray(same & tri, BF16)


def _meta_state_body(k_ref, v_ref, z_ref, wg_ref, bg_ref, tri_ref, s_ref):
    la = _gate_log_decay(z_ref[...], wg_ref[...], bg_ref[...])
    bc = _chunk_cumsum(tri_ref[...], la)
    bl = bc[N_META - 1:N_META]
    ke = (k_ref[...] * jnp.exp(bl - bc)).astype(BF16)
    v = v_ref[...].astype(BF16)
    for h in range(GLA_HEADS):
        s_ref[h] = lax.dot_general(v[:, h * HEAD_V:(h + 1) * HEAD_V], ke[:, h * HEAD_K:(h + 1) * HEAD_K],
                                   (((0,), (0,)), ((), ())), preferred_element_type=F32)


def _meta_state(k_m, v_m, z_m, wg2, bg):
    return pl.pallas_call(
        _meta_state_body,
        out_shape=jax.ShapeDtypeStruct((GLA_HEADS, HEAD_V, HEAD_K), F32),
        name="gla_meta_state",
    )(k_m, v_m, z_m, wg2, bg, _block_tri(N_META, False))


def _gla_body(*refs, reverse, cb, has_prev):
    if has_prev:
        qk_ref, v_ref, z_ref, wg_ref, bg_ref, tri_ref, s0_ref, prev_ref, o_ref, st_ref = refs
    else:
        qk_ref, v_ref, z_ref, wg_ref, bg_ref, tri_ref, s0_ref, o_ref, st_ref = refs
        prev_ref = None

    @pl.when(pl.program_id(1) == 0)
    def _():
        st_ref[...] = s0_ref[...]

    nrow = cb * CHUNK
    row = lax.broadcasted_iota(jnp.int32, (nrow, nrow), 0)
    col = lax.broadcasted_iota(jnp.int32, (nrow, nrow), 1)
    shift = CHUNK.bit_length() - 1
    same_chunk = (row >> shift) == (col >> shift)
    msk = jnp.logical_and(same_chunk, (col > row) if reverse else (col <= row))
    scale = HEAD_K ** -0.5
    contract_last = (((1,), (1,)), ((), ()))
    contract_first = (((0,), (0,)), ((), ()))
    la = _gate_log_decay(z_ref[0], wg_ref[...], bg_ref[...])
    bc = _chunk_cumsum(tri_ref[...], la)
    chunks = [slice(c * CHUNK, (c + 1) * CHUNK) for c in range(cb)]
    edge = [bc[c * CHUNK:c * CHUNK + 1] if reverse else bc[(c + 1) * CHUNK - 1:(c + 1) * CHUNK]
            for c in range(cb)]
    bl = jnp.concatenate([jnp.broadcast_to(e, (CHUNK, GLA_KEY)) for e in edge], axis=0)
    q = qk_ref[0, :, 0:GLA_KEY] * scale
    k = qk_ref[0, :, GLA_KEY:2 * GLA_KEY]
    qd = (q * jnp.exp(bc)).astype(BF16)
    kd = (k * jnp.exp(-bc)).astype(BF16)
    ke = (k * jnp.exp(bl - bc)).astype(BF16)
    dec = [jnp.exp(e) for e in edge]
    v = v_ref[0].astype(BF16)
    order = range(cb - 1, -1, -1) if reverse else range(cb)
    for h in range(GLA_HEADS):
        ks = slice(h * HEAD_K, (h + 1) * HEAD_K)
        vs = slice(h * HEAD_V, (h + 1) * HEAD_V)
        sc = lax.dot_general(qd[:, ks], kd[:, ks], contract_last, preferred_element_type=F32)
        sc = jnp.where(msk, sc, 0.0).astype(BF16)
        o_intra = jnp.dot(sc, v[:, vs], preferred_element_type=F32)
        st = st_ref[h]
        for c in order:
            rows = chunks[c]
            o = o_intra[rows] + lax.dot_general(qd[rows, ks], st.astype(BF16), contract_last,
                                                preferred_element_type=F32)
            upd = lax.dot_general(v[rows, vs], ke[rows, ks], contract_first, preferred_element_type=F32)
            st = st * dec[c][:, ks] + upd
            if prev_ref is not None:
                o = prev_ref[0, rows, vs] + o
            o_ref[0, rows, vs] = o
        st_ref[h] = st


def _gla(qk, v, z, wg2, bg, s0, prev, reverse):
    b, s, _ = qk.shape
    cb = 4
    rows = cb * CHUNK
    nb = s // rows
    btri = _block_tri(rows, reverse)
    if reverse:
        blk = lambda bi, i: (bi, nb - 1 - i, 0)
    else:
        blk = lambda bi, i: (bi, i, 0)
    in_specs = [pl.BlockSpec((1, rows, 2 * GLA_KEY), blk),
                pl.BlockSpec((1, rows, GLA_WIDTH), blk),
                pl.BlockSpec((1, rows, LANES), blk),
                _resident(wg2.shape), _resident(bg.shape), _resident(btri.shape), _resident(s0.shape)]
    args = [qk, v, z, wg2, bg, btri, s0]
    if prev is not None:
        in_specs.append(pl.BlockSpec((1, rows, GLA_WIDTH), blk))
        args.append(prev)
    return pl.pallas_call(
        functools.partial(_gla_body, reverse=reverse, cb=cb, has_prev=prev is not None),
        grid=(b, nb),
        in_specs=in_specs,
        out_specs=pl.BlockSpec((1, rows, GLA_WIDTH), blk),
        out_shape=jax.ShapeDtypeStruct((b, s, GLA_WIDTH), F32),
        scratch_shapes=[pltpu.VMEM((GLA_HEADS, HEAD_V, HEAD_K), F32)],
        compiler_params=_cparams(("parallel", "arbitrary")),
        name="gla_bwd" if reverse else "gla_fwd",
    )(*args)


def _dft_factors(length):
    best = None
    for n2 in range(8, length, 8):
        if length % n2 == 0:
            n1 = length // n2
            if best is None or n2 + 2 * n1 < best[1] + 2 * best[0]:
                best = (n1, n2)
    assert best is not None, length
    return best


def _dft_plan(length, channels):
    n1, n2 = _dft_factors(length)
    kc = _round_up(n1, 16)
    pitch = _round_up(n1, 8)
    if (pitch // 8) % 2 == 0:
        pitch += 8
    rows = _round_up(n2 * pitch + max(0, kc - pitch), 8)
    cw = LANES
    assert channels % cw == 0
    group = 2
    return n1, n2, kc, pitch, rows, cw, group


def _round_up(x, m):
    return (x + m - 1) // m * m


def _dft_tables(length, n1, n2, n1p):
    two_pi = 2.0 * math.pi
    r = np.arange(n2)
    pa = ((r[:, None] + N_META) * r[None, :]) % n2
    ang = two_pi * pa / n2
    a_stack = np.concatenate([np.cos(ang), -np.sin(ang)], axis=0)
    r1 = np.arange(n1)
    pb = (r1[:, None] * (r1[None, :] + N_META)) % n1
    angb = two_pi * pb / n1
    br = np.zeros((n1p, n1p)); bi = np.zeros((n1p, n1p))
    br[:n1, :n1] = np.cos(angb); bi[:n1, :n1] = -np.sin(angb)
    pt = ((r[:, None] + N_META) * (r1[None, :] + N_META)) % length
    angt = two_pi * pt / length
    twr = np.zeros((n2, 1, n1p)); twi = np.zeros((n2, 1, n1p))
    twr[:, 0, :n1] = np.cos(angt); twi[:, 0, :n1] = -np.sin(angt)
    return tuple(jnp.asarray(t, F32) for t in (a_stack, br, bi, twr, twi))


def _seqdft_body(x_ref, a_ref, br_ref, bi_ref, twr_ref, twi_ref, vr_any, vi_any,
                 ur_ref, ui_ref, vr_ref, vi_ref, sems, *, n1, n2, kc, pitch, cw, group):
    bidx = pl.program_id(0)
    cidx = pl.program_id(1)
    ncb = pl.num_programs(1)
    step = bidx * ncb + cidx
    last = pl.num_programs(0) * ncb - 1

    def writeback(bb, cc, start):
        col = pl.multiple_of(cc * cw, cw)
        for part, (src, dst) in enumerate(((vr_ref, vr_any), (vi_ref, vi_any))):
            cp = pltpu.make_async_copy(src, dst.at[bb, :, pl.ds(col, cw)], sems.at[part])
            if start:
                cp.start()
            else:
                cp.wait()

    @pl.when(step == 0)
    def _():
        ur_ref[...] = jnp.zeros(ur_ref.shape, F32)
        ui_ref[...] = jnp.zeros(ui_ref.shape, F32)

    a = a_ref[...].astype(BF16)

    def slab_group(m1, cnt):
        slabs = [x_ref[0, pl.ds(m1 + t, n2, stride=n1), :] for t in range(cnt)]
        rhs = (slabs[0] if cnt == 1 else jnp.concatenate(slabs, axis=1)).astype(BF16)
        p = jnp.dot(a, rhs, preferred_element_type=F32)
        for t in range(cnt):
            ur_ref[pl.ds(m1 + t, n2, stride=pitch), :] = p[:n2, t * cw:(t + 1) * cw]
            ui_ref[pl.ds(m1 + t, n2, stride=pitch), :] = p[n2:, t * cw:(t + 1) * cw]

    def step_a(it, carry):
        slab_group(it * group, group)
        return carry

    lax.fori_loop(0, n1 // group, step_a, 0, unroll=4)
    if n1 % group:
        slab_group((n1 // group) * group, n1 % group)

    br = br_ref[...]
    bi = bi_ref[...]

    @pl.when(step > 0)
    def _():
        prev_c = jnp.where(cidx == 0, ncb - 1, cidx - 1)
        prev_b = jnp.where(cidx == 0, bidx - 1, bidx)
        writeback(prev_b, prev_c, False)

    def step_c(j, carry):
        tr = twr_ref[j]
        ti = twi_ref[j]
        fm = jnp.concatenate([br * tr - bi * ti, br * ti + bi * tr], axis=0).astype(BF16)
        start = pl.multiple_of(j * pitch, 8)
        rhs = jnp.concatenate([ur_ref[pl.ds(start, kc), :], ui_ref[pl.ds(start, kc), :]],
                              axis=1).astype(BF16)
        r = jnp.dot(fm, rhs, preferred_element_type=F32)
        vr_ref[pl.ds(j, n1, stride=n2), :] = r[:n1, :cw] - r[kc:kc + n1, cw:]
        vi_ref[pl.ds(j, n1, stride=n2), :] = r[:n1, cw:] + r[kc:kc + n1, :cw]
        return carry

    lax.fori_loop(0, n2, step_c, 0, unroll=4)

    writeback(bidx, cidx, True)

    @pl.when(step == last)
    def _():
        writeback(bidx, cidx, False)


def _seqdft(four):
    b, length, c = four.shape
    n1, n2, kc, pitch, rows, cw, group = _dft_plan(length, c)
    tables = _dft_tables(length, n1, n2, kc)
    out = jax.ShapeDtypeStruct((b, length, c), F32)
    return pl.pallas_call(
        functools.partial(_seqdft_body, n1=n1, n2=n2, kc=kc, pitch=pitch, cw=cw, group=group),
        grid=(b, c // cw),
        in_specs=[pl.BlockSpec((1, length, cw), lambda bi, ci: (bi, 0, ci), pipeline_mode=pl.Buffered(1))]
                 + [_resident(t.shape) for t in tables],
        out_specs=[pl.BlockSpec(memory_space=pl.ANY), pl.BlockSpec(memory_space=pl.ANY)],
        out_shape=[out, out],
        scratch_shapes=[pltpu.VMEM((rows, cw), F32), pltpu.VMEM((rows, cw), F32),
                        pltpu.VMEM((length, cw), F32), pltpu.VMEM((length, cw), F32),
                        pltpu.SemaphoreType.DMA((2,))],
        compiler_params=_cparams(("arbitrary", "arbitrary")),
        name="seqdft",
    )(four, *tables)


def _outproj_body(o_ref, g_ref, vr_ref, vi_ref, x_ref, gw_ref, cw_ref, sw_ref, wo_ref, n2w_ref, wr_ref, br_ref,
                  base_in_ref, h_ref, hn_ref, route_ref, cnt_ref, base_ref, *, fnorm):
    first = jnp.logical_and(pl.program_id(0) == 0, pl.program_id(1) == 0)

    @pl.when(first)
    def _():
        base_ref[...] = base_in_ref[...]

    tm = o_ref.shape[1]
    parts = []
    for h in range(GLA_HEADS):
        vs = slice(h * HEAD_V, (h + 1) * HEAD_V)
        o = o_ref[0, :, vs]
        o = o * lax.rsqrt(jnp.mean(o * o, axis=-1, keepdims=True) + EPS)
        o = o * gw_ref[:, vs]
        g = g_ref[0, :, vs]
        parts.append((o * (g * (1.0 / (1.0 + jnp.exp(-g))))).astype(BF16))
    cw = cw_ref[...].astype(BF16)
    sw = sw_ref[...].astype(BF16)
    for gi in range(FNET_WIDTH // FNET_GROUP_W):
        cs = slice(gi * FNET_GROUP_W, (gi + 1) * FNET_GROUP_W)
        vr = vr_ref[0, :, cs].astype(BF16)
        vi = vi_ref[0, :, cs].astype(BF16)
        f = (jnp.dot(vr, cw, preferred_element_type=F32) + jnp.dot(vi, sw, preferred_element_type=F32))
        parts.append((f * fnorm).astype(BF16))
    merged = jnp.concatenate(parts, axis=-1)
    h1 = x_ref[0] + jnp.dot(merged, wo_ref[...], preferred_element_type=F32)
    h_ref[0] = h1
    hn = h1 * lax.rsqrt(jnp.mean(h1 * h1, axis=-1, keepdims=True) + EPS) * n2w_ref[...]
    hn_ref[0] = hn

    hn_hi = hn.astype(BF16)
    hn_lo = (hn - hn_hi.astype(F32)).astype(BF16)
    pp = jnp.dot(jnp.concatenate([hn_hi, hn_lo], axis=0), wr_ref[...], preferred_element_type=F32)
    logits = ((pp[:tm, :LANES] + pp[:tm, LANES:]) + (pp[tm:, :LANES] + pp[tm:, LANES:])) + br_ref[...]
    lane = lax.broadcasted_iota(jnp.int32, (tm, LANES), 1)
    neg = jnp.float32(-jnp.inf)
    gl = jnp.where(lane < N_GROUPS, logits, neg)
    gmax = jnp.max(gl, axis=-1, keepdims=True)
    grp = jnp.min(jnp.where(gl == gmax, lane, LANES), axis=-1, keepdims=True)
    grp_w = 1.0 / jnp.sum(jnp.exp(gl - gmax), axis=-1, keepdims=True)
    elane = lane - N_GROUPS
    member = jnp.logical_and(jnp.logical_and(elane >= 0, elane < N_EXPERTS),
                             (elane >> 3) == grp)
    el = jnp.where(member, logits, neg)
    m0 = jnp.max(el, axis=-1, keepdims=True)
    i0 = jnp.min(jnp.where(el == m0, lane, LANES), axis=-1, keepdims=True)
    el2 = jnp.where(lane == i0, neg, el)
    m1 = jnp.max(el2, axis=-1, keepdims=True)
    i1 = jnp.min(jnp.where(el2 == m1, lane, LANES), axis=-1, keepdims=True)
    p1 = jnp.exp(m1 - m0)
    den = 1.0 + p1
    w0 = (1.0 / den) * grp_w
    w1 = (p1 / den) * grp_w
    e0 = i0 - N_GROUPS
    e1 = i1 - N_GROUPS

    oh0 = (lane == e0).astype(BF16)
    oh1 = (lane == e1).astype(BF16)
    r_i = lax.broadcasted_iota(jnp.int32, (tm, tm), 0)
    c_i = lax.broadcasted_iota(jnp.int32, (tm, tm), 1)
    ltri = (c_i < r_i).astype(BF16)
    c0 = jnp.dot(ltri, oh0, preferred_element_type=F32)
    c1 = jnp.dot(ltri, oh1, preferred_element_type=F32)
    oh0f = oh0.astype(F32)
    oh1f = oh1.astype(F32)
    tot0 = jnp.sum(oh0f, axis=0, keepdims=True)
    tot1 = jnp.sum(oh1f, axis=0, keepdims=True)
    base = base_ref[...]
    rank0 = jnp.sum(oh0f * (c0 + base), axis=-1, keepdims=True)
    rank1 = jnp.sum(oh1f * (c1 + base + tot0), axis=-1, keepdims=True)
    new_base = base + tot0 + tot1
    base_ref[...] = new_base
    cnt_ref[...] = new_base

    route = jnp.where(lane == 0, e0.astype(F32),
            jnp.where(lane == 1, e1.astype(F32),
            jnp.where(lane == 2, w0,
            jnp.where(lane == 3, w1,
            jnp.where(lane == 4, rank0,
            jnp.where(lane == 5, rank1, 0.0))))))
    route_ref[0] = route


def _outproj(o, g, vr, vi, x, gw, cw, sw, wo, n2w, wr, br, base_in, length):
    b, s, d = x.shape
    tm = _row_tile(s, 512)
    fnorm = 1.0 / math.sqrt(length * FNET_GROUP_W)
    row = lambda w: pl.BlockSpec((1, tm, w), lambda bi, i: (bi, i, 0))
    return pl.pallas_call(
        functools.partial(_outproj_body, fnorm=fnorm),
        grid=(b, s // tm),
        in_specs=[row(GLA_WIDTH), row(GLA_WIDTH), row(FNET_WIDTH), row(FNET_WIDTH), row(d),
                  _resident(gw.shape), _resident(cw.shape), _resident(sw.shape), _resident(wo.shape),
                  _resident(n2w.shape), _resident(wr.shape), _resident(br.shape), _resident(base_in.shape)],
        out_specs=[row(d), row(d), row(LANES), pl.BlockSpec((1, LANES), lambda bi, i: (0, 0))],
        out_shape=[jax.ShapeDtypeStruct((b, s, d), F32), jax.ShapeDtypeStruct((b, s, d), F32),
                   jax.ShapeDtypeStruct((b, s, LANES), F32), jax.ShapeDtypeStruct((1, LANES), F32)],
        scratch_shapes=[pltpu.VMEM((1, LANES), F32)],
        compiler_params=_cparams(("arbitrary", "arbitrary")),
        name="outproj_router",
    )(o, g, vr, vi, x, gw, cw, sw, wo, n2w, wr, br, base_in)


DISPATCH_WINDOW = 256

def _dispatch_body(slot_ref, zblk_ref, zval_ref, *refs, counts, win):
    srcs = refs[:len(counts)]
    xs_out, zero_ref, tile_ref, zsem, lsems, ssems = refs[len(counts):]
    step = pl.program_id(0)
    nsteps = sum(counts) // win

    @pl.when(step == 0)
    def _():
        zero_ref[...] = jnp.zeros(zero_ref.shape, zero_ref.dtype)

        def zcopy(i):
            row = pl.multiple_of(zblk_ref[i] * MOE_BM, MOE_BM)
            return pltpu.make_async_copy(zero_ref, xs_out.at[pl.ds(row, MOE_BM)], zsem)

        for i in range(2 * N_EXPERTS):
            @pl.when(zval_ref[i] != 0)
            def _():
                zcopy(i).start()
        for i in range(2 * N_EXPERTS):
            @pl.when(zval_ref[i] != 0)
            def _():
                zcopy(i).wait()

    def load(w, start):
        lo = 0
        for src, n in zip(srcs, counts):
            hi = lo + n // win

            @pl.when(jnp.logical_and(w >= lo, w < hi))
            def _(src=src, lo=lo):
                row0 = pl.multiple_of((w - lo) * win, 8)
                b = lax.rem(w, 3)
                cp = pltpu.make_async_copy(src.at[pl.ds(row0, win)], tile_ref.at[b], lsems.at[b])
                if start:
                    cp.start()
                else:
                    cp.wait()

            lo = hi

    def scatter(w, start):
        b = lax.rem(w, 3)
        sem = ssems.at[lax.rem(w, 2)]
        tok0 = w * win

        def body(r, c):
            for k in range(TOP_K):
                s = slot_ref[2 * (tok0 + r) + k]
                cp = pltpu.make_async_copy(tile_ref.at[b, pl.ds(r, 1)], xs_out.at[pl.ds(s, 1)], sem)
                if start:
                    cp.start()
                else:
                    cp.wait()
            return c

        lax.fori_loop(0, win, body, 0, unroll=8)

    @pl.when(step == 0)
    def _():
        load(step, True)

    @pl.when(step + 1 < nsteps)
    def _():
        load(step + 1, True)

    load(step, False)
    scatter(step, True)

    @pl.when(step > 0)
    def _():
        scatter(step - 1, False)

    @pl.when(step == nsteps - 1)
    def _():
        scatter(step, False)


def _dispatch(slots, zblk, zval, sources, n_rows):
    d = sources[0].shape[1]
    dtype = sources[0].dtype
    counts = tuple(s.shape[0] for s in sources)
    win = functools.reduce(math.gcd, counts + (DISPATCH_WINDOW,))
    return pl.pallas_call(
        functools.partial(_dispatch_body, counts=counts, win=win),
        grid_spec=pltpu.PrefetchScalarGridSpec(
            num_scalar_prefetch=3,
            grid=(sum(counts) // win,),
            in_specs=[pl.BlockSpec(memory_space=pl.ANY) for _ in sources],
            out_specs=pl.BlockSpec(memory_space=pl.ANY),
            scratch_shapes=[pltpu.VMEM((MOE_BM, d), dtype), pltpu.VMEM((3, win, d), dtype),
                            pltpu.SemaphoreType.DMA(()), pltpu.SemaphoreType.DMA((3,)),
                            pltpu.SemaphoreType.DMA((2,))]),
        out_shape=jax.ShapeDtypeStruct((n_rows, d), dtype),
        compiler_params=_cparams(("arbitrary",)),
        name="moe_dispatch",
    )(slots, zblk, zval, *sources)


def _experts_body(be_ref, nu_ref, x_ref, wg_ref, wu_ref, wd_ref, y_ref, wgb_ref, wub_ref, wdb_ref):
    i = pl.program_id(0)
    used = i < nu_ref[0]
    ic = jnp.minimum(i, nu_ref[0] - 1)
    fresh = jnp.logical_or(i == 0, be_ref[ic] != be_ref[jnp.maximum(ic - 1, 0)])

    @pl.when(jnp.logical_and(used, fresh))
    def _():
        wgb_ref[...] = wg_ref[0].astype(BF16)
        wub_ref[...] = wu_ref[0].astype(BF16)
        wdb_ref[...] = wd_ref[0].astype(BF16)

    @pl.when(used)
    def _():
        xb = x_ref[...].astype(BF16)
        gate = jnp.dot(xb, wgb_ref[...], preferred_element_type=F32)
        up = jnp.dot(xb, wub_ref[...], preferred_element_type=F32)
        hid = (gate * (1.0 / (1.0 + jnp.exp(-gate))) * up).astype(BF16)
        y_ref[...] = jnp.dot(hid, wdb_ref[...], preferred_element_type=F32)

    @pl.when(jnp.logical_not(used))
    def _():
        y_ref[...] = jnp.zeros(y_ref.shape, F32)


def _experts(block_expert, n_used, xs, wg, wu, wd):
    p, d = xs.shape
    nb = p // MOE_BM
    blk = lambda i, be, nu: (jnp.minimum(i, nu[0] - 1), 0)
    wsel = lambda i, be, nu: (be[jnp.minimum(i, nu[0] - 1)], 0, 0)
    return pl.pallas_call(
        _experts_body,
        grid_spec=pltpu.PrefetchScalarGridSpec(
            num_scalar_prefetch=2,
            grid=(nb,),
            in_specs=[pl.BlockSpec((MOE_BM, d), blk),
                      pl.BlockSpec((1, d, EXPERT_FF), wsel),
                      pl.BlockSpec((1, d, EXPERT_FF), wsel),
                      pl.BlockSpec((1, EXPERT_FF, d), wsel)],
            out_specs=pl.BlockSpec((MOE_BM, d), lambda i, be, nu: (i, 0)),
            scratch_shapes=[pltpu.VMEM((d, EXPERT_FF), BF16), pltpu.VMEM((d, EXPERT_FF), BF16),
                            pltpu.VMEM((EXPERT_FF, d), BF16)]),
        out_shape=jax.ShapeDtypeStruct((p, d), F32),
        compiler_params=_cparams(("arbitrary",)),
        name="moe_experts",
    )(block_expert, n_used, xs, wg, wu, wd)


def _combine_body(slot_ref, h_ref, route_ref, ys_any, fw_ref, o_ref, ybuf_ref, sems, *, nt):
    tm = h_ref.shape[0]
    i = pl.program_id(0)

    def copy(tile, r, k):
        buf = lax.rem(tile, 2)
        s = slot_ref[2 * (tile * tm + r) + k]
        return pltpu.make_async_copy(ys_any.at[pl.ds(s, 1)], ybuf_ref.at[buf, k, pl.ds(r, 1)], sems.at[buf])

    def issue_tile(tile):
        def body(r, carry):
            copy(tile, r, 0).start()
            copy(tile, r, 1).start()
            return carry
        lax.fori_loop(0, tm, body, 0, unroll=8)

    @pl.when(i == 0)
    def _():
        issue_tile(i)

    @pl.when(i + 1 < nt)
    def _():
        issue_tile(i + 1)

    def drain(r, carry):
        copy(i, r, 0).wait()
        copy(i, r, 1).wait()
        return carry

    lax.fori_loop(0, tm, drain, 0, unroll=8)

    buf = lax.rem(i, 2)
    w0 = route_ref[:, 2:3]
    w1 = route_ref[:, 3:4]
    h = h_ref[...] + (ybuf_ref[buf, 0] * w0 + ybuf_ref[buf, 1] * w1)
    o_ref[...] = h * lax.rsqrt(jnp.mean(h * h, axis=-1, keepdims=True) + EPS) * fw_ref[...]


def _combine(slots, h2d, route2d, ys, fw):
    t, d = h2d.shape
    tm = _row_tile(t, 256)
    nt = t // tm
    return pl.pallas_call(
        functools.partial(_combine_body, nt=nt),
        grid_spec=pltpu.PrefetchScalarGridSpec(
            num_scalar_prefetch=1,
            grid=(nt,),
            in_specs=[pl.BlockSpec((tm, d), lambda i, sl: (i, 0)),
                      pl.BlockSpec((tm, LANES), lambda i, sl: (i, 0)),
                      pl.BlockSpec(memory_space=pl.ANY),
                      pl.BlockSpec((1, d), lambda i, sl: (0, 0))],
            out_specs=pl.BlockSpec((tm, d), lambda i, sl: (i, 0)),
            scratch_shapes=[pltpu.VMEM((2, TOP_K, tm, d), F32), pltpu.SemaphoreType.DMA((2,))]),
        out_shape=jax.ShapeDtypeStruct((t, d), F32),
        compiler_params=_cparams(("arbitrary",)),
        name="moe_combine",
    )(slots, h2d, route2d, ys, fw)


def _prep_weights(norm1_w, w_in, w_gate_up, b_gate_up, gla_norm_w, w_out, norm2_w, w_router_group,
                  b_router_group, w_router_expert, b_router_expert, w_expert_gate, w_expert_up,
                  w_expert_down, final_norm_w):
    w = w_in[0]
    wz = jnp.pad(w[:, OFF_ZF:OFF_F], ((0, 0), (0, LANES - 2 * GATE_RANK)))
    wg_f = jnp.pad(w_gate_up[0, 0], ((0, LANES - GATE_RANK), (0, 0)))
    wg_b = jnp.pad(w_gate_up[0, 1], ((GATE_RANK, LANES - 2 * GATE_RANK), (0, 0)))
    rpad = LANES - N_GROUPS - N_EXPERTS
    wr = jnp.pad(jnp.concatenate([w_router_group[0], w_router_expert[0]], axis=1), ((0, 0), (0, rpad)))
    br = jnp.pad(jnp.concatenate([b_router_group[0], b_router_expert[0]]), (0, rpad))[None]
    k = np.arange(FNET_GROUP_W)
    ang = 2.0 * math.pi * ((k[:, None] * k[None, :]) % FNET_GROUP_W) / FNET_GROUP_W
    return dict(
        nw1=norm1_w[0][None], wqk=w[:, :OFF_V].astype(BF16), wv=w[:, OFF_V:OFF_G].astype(BF16), wg=w[:, OFF_G:OFF_ZF].astype(BF16),
        wz=wz.astype(BF16), wf=w[:, OFF_F:].astype(BF16),
        wg_f=_split_bf16(wg_f), wg_b=_split_bf16(wg_b), bg_f=b_gate_up[0, 0][None], bg_b=b_gate_up[0, 1][None],
        gw=gla_norm_w[0][None], cw=jnp.asarray(np.cos(ang), F32), sw=jnp.asarray(np.sin(ang), F32),
        wo=w_out[0].astype(BF16), nw2=norm2_w[0][None], wr=_split_bf16(wr), br=br,
        weg=w_expert_gate[0], weu=w_expert_up[0], wed=w_expert_down[0],
        fw=final_norm_w[None])


def _split_bf16(w):
    hi = w.astype(BF16)
    lo = (w - hi.astype(F32)).astype(BF16)
    return jnp.concatenate([hi, lo], axis=1)


def _mixer_and_route(x, meta, pw, base_in):
    b, s, d = x.shape
    length = s + N_META
    qk, v, g, z, four = _inproj(x, meta["four"], pw["nw1"], pw["wqk"], pw["wv"], pw["wg"], pw["wz"], pw["wf"])
    s0 = _meta_state(meta["k"], meta["v"], meta["z"], pw["wg_f"], pw["bg_f"])
    o_f = _gla(qk, v, z, pw["wg_f"], pw["bg_f"], s0, None, reverse=False)
    o = _gla(qk, v, z, pw["wg_b"], pw["bg_b"], jnp.zeros_like(s0), o_f, reverse=True)
    vr, vi = _seqdft(four)
    return _outproj(o, g, vr, vi, x, pw["gw"], pw["cw"], pw["sw"], pw["wo"], pw["nw2"], pw["wr"], pw["br"],
                    base_in, length)


def kernel(x_prompt, x_sample, meta_tokens, norm1_w, w_in, w_gate_up, b_gate_up, gla_norm_w, w_out, norm2_w,
           w_router_group, b_router_group, w_router_expert, b_router_expert, w_expert_gate, w_expert_up,
           w_expert_down, final_norm_w):
    pw = _prep_weights(norm1_w, w_in, w_gate_up, b_gate_up, gla_norm_w, w_out, norm2_w, w_router_group,
                       b_router_group, w_router_expert, b_router_expert, w_expert_gate, w_expert_up,
                       w_expert_down, final_norm_w)
    qk_m, v_m, _, z_m, four_m = _inproj(meta_tokens[None], None, pw["nw1"], pw["wqk"], pw["wv"], pw["wg"],
                                         pw["wz"], pw["wf"])
    meta = dict(k=qk_m[0, :, GLA_KEY:], v=v_m[0], z=z_m[0], four=four_m)

    xs_in = (x_prompt, x_sample)
    base = jnp.zeros((1, LANES), F32)
    mixed = []
    for x in xs_in:
        h, hn, route, base = _mixer_and_route(x, meta, pw, base)
        mixed.append((h, hn, route))

    counts = base[0, :N_EXPERTS].astype(jnp.int32)
    padded = (counts + MOE_BM - 1) // MOE_BM * MOE_BM
    pends = jnp.cumsum(padded)
    pstart = (pends - padded).astype(F32)
    n_assign = sum(x.shape[0] * x.shape[1] for x in xs_in) * TOP_K
    nb = -(-(n_assign + N_EXPERTS * (MOE_BM - 1)) // MOE_BM)
    n_used = (pends[-1:] // MOE_BM).astype(jnp.int32)
    blk_row = jnp.arange(nb, dtype=jnp.int32) * MOE_BM
    block_expert = jnp.minimum(jnp.sum((pends[None, :] <= blk_row[:, None]).astype(jnp.int32), axis=1),
                               N_EXPERTS - 1)
    tail = n_used[0] + jnp.arange(N_EXPERTS, dtype=jnp.int32)
    zblk = jnp.concatenate([jnp.maximum(pends // MOE_BM - 1, 0), jnp.minimum(tail, nb - 1)]).astype(jnp.int32)
    zval = jnp.concatenate([padded > 0, tail < nb]).astype(jnp.int32)

    slot_list = []
    for (h, hn, route) in mixed:
        t = h.shape[0] * h.shape[1]
        r2 = route.reshape(t, LANES)
        e = r2[:, 0:2].astype(jnp.int32)
        onehot = (e[:, :, None] == jnp.arange(N_EXPERTS, dtype=jnp.int32)).astype(F32)
        slots = (jnp.sum(onehot * pstart, axis=-1) + r2[:, 4:6]).astype(jnp.int32).reshape(-1)
        slot_list.append(slots)
    xs = _dispatch(jnp.concatenate(slot_list), zblk, zval,
                   [hn.reshape(-1, hn.shape[-1]) for (_, hn, _) in mixed], nb * MOE_BM)

    ys = _experts(block_expert, n_used, xs, pw["weg"], pw["weu"], pw["wed"])

    outs = []
    for x, (h, hn, route), slots in zip(xs_in, mixed, slot_list):
        t = h.shape[0] * h.shape[1]
        y = _combine(slots, h.reshape(t, D_MODEL), route.reshape(t, LANES), ys, pw["fw"])
        outs.append(y.reshape(x.shape))
    return tuple(outs)
```

```python
import functools
import math

import numpy as np
import jax
import jax.numpy as jnp
from jax import lax
from jax.experimental import pallas as pl
from jax.experimental.pallas import tpu as pltpu

F32 = jnp.float32
BF16 = jnp.bfloat16

D_MODEL = 2048
N_META = 16
GLA_WIDTH = 1024
FNET_WIDTH = 1024
GLA_HEADS = 4
HEAD_V = 256
GLA_KEY = 512
HEAD_K = 128
GATE_RANK = 16
GATE_TAU = 16.0
CHUNK = 64
FNET_GROUP_W = 256
N_GROUPS = 4
EXPERTS_PER_GROUP = 8
N_EXPERTS = 32
TOP_K = 2
EXPERT_FF = 512
EPS = 1e-6
OFF_K = 512
OFF_V = 1024
OFF_G = 2048
OFF_ZF = 3072
OFF_F = 3104

LANES = 128
MOE_BM = 512
VMEM_LIMIT = 56 * 1024 * 1024


def _cparams(semantics, vmem=VMEM_LIMIT):
    return pltpu.CompilerParams(dimension_semantics=semantics, vmem_limit_bytes=vmem)


def _resident(shape):
    nd = len(shape)
    return pl.BlockSpec(shape, lambda *_: (0,) * nd, pipeline_mode=pl.Buffered(1))


def _row_tile(n, target):
    t = min(n, target)
    while n % t:
        t -= 8
    return t


def _inproj_body(x_ref, fm_ref, nw_ref, wqk_ref, wv_ref, wg_ref, wz_ref, wf_ref,
                 qk_ref, v_ref, g_ref, z_ref, f_ref, *, nt):
    i = pl.program_id(1)

    @pl.when(i < nt)
    def _():
        x = x_ref[0]
        y = x * lax.rsqrt(jnp.mean(x * x, axis=-1, keepdims=True) + EPS)
        yb = (y * nw_ref[...]).astype(BF16)
        qk_ref[0] = jnp.dot(yb, wqk_ref[...], preferred_element_type=F32)
        v_ref[0] = jnp.dot(yb, wv_ref[...], preferred_element_type=F32).astype(BF16)
        g_ref[0] = jnp.dot(yb, wg_ref[...], preferred_element_type=F32)
        z_ref[0] = jnp.dot(yb, wz_ref[...], preferred_element_type=F32)
        f_ref[0] = jnp.dot(yb, wf_ref[...], preferred_element_type=F32)

    @pl.when(i == nt)
    def _():
        f_ref[0, 0:N_META, :] = fm_ref[0]


def _inproj(x, four_meta, nw, wqk, wv, wg, wz, wf):
    b, s, d = x.shape
    tm = _row_tile(s, 512)
    nt = s // tm
    extra = 0 if four_meta is None else 1
    if four_meta is None:
        four_meta = jnp.zeros((1, N_META, FNET_WIDTH), F32)
    row = lambda w: pl.BlockSpec((1, tm, w), lambda bi, i: (bi, jnp.minimum(i, nt - 1), 0))
    return pl.pallas_call(
        functools.partial(_inproj_body, nt=nt),
        grid=(b, nt + extra),
        in_specs=[row(d), _resident(four_meta.shape), _resident((1, d)), _resident(wqk.shape),
                  _resident(wv.shape), _resident(wg.shape), _resident(wz.shape), _resident(wf.shape)],
        out_specs=[row(2 * GLA_KEY), row(GLA_WIDTH), row(GLA_WIDTH), row(LANES),
                   pl.BlockSpec((1, tm, FNET_WIDTH), lambda bi, i: (bi, i, 0))],
        out_shape=[jax.ShapeDtypeStruct((b, s, 2 * GLA_KEY), F32),
                   jax.ShapeDtypeStruct((b, s, GLA_WIDTH), BF16),
                   jax.ShapeDtypeStruct((b, s, GLA_WIDTH), F32),
                   jax.ShapeDtypeStruct((b, s, LANES), F32),
                   jax.ShapeDtypeStruct((b, s + extra * N_META, FNET_WIDTH), F32)],
        compiler_params=_cparams(("parallel", "arbitrary")),
        name="inproj",
    )(x, four_meta, nw, wqk, wv, wg, wz, wf)


def _log_sigmoid(x):
    return jnp.minimum(x, 0.0) - jnp.log1p(jnp.exp(-jnp.abs(x)))


def _gate_log_decay(z, wg2, bg):
    n = z.shape[0]
    z_hi = z.astype(BF16)
    z_lo = (z - z_hi.astype(F32)).astype(BF16)
    pp = jnp.dot(jnp.concatenate([z_hi, z_lo], axis=0), wg2, preferred_element_type=F32)
    pre = ((pp[:n, :GLA_KEY] + pp[:n, GLA_KEY:]) + (pp[n:, :GLA_KEY] + pp[n:, GLA_KEY:])) + bg
    return _log_sigmoid(pre) * (1.0 / GATE_TAU)


def _chunk_cumsum(btri, la):
    la_hi = la.astype(BF16)
    la_lo = (la - la_hi.astype(F32)).astype(BF16)
    pp = jnp.dot(btri, jnp.concatenate([la_hi, la_lo], axis=1), preferred_element_type=F32)
    return pp[:, :GLA_KEY] + pp[:, GLA_KEY:]


def _block_tri(rows, reverse):
    r = np.arange(rows)
    same = (r[:, None] // CHUNK) == (r[None, :] // CHUNK)
    tri = (r[None, :] >= r[:, None]) if reverse else (r[None, :] <= r[:, None])
    return jnp.asarray(same & tri, BF16)


def _meta_state_body(k_ref, v_ref, z_ref, wg_ref, bg_ref, tri_ref, s_ref):
    la = _gate_log_decay(z_ref[...], wg_ref[...], bg_ref[...])
    bc = _chunk_cumsum(tri_ref[...], la)
    bl = bc[N_META - 1:N_META]
    ke = (k_ref[...] * jnp.exp(bl - bc)).astype(BF16)
    v = v_ref[...].astype(BF16)
    for h in range(GLA_HEADS):
        s_ref[h] = lax.dot_general(v[:, h * HEAD_V:(h + 1) * HEAD_V], ke[:, h * HEAD_K:(h + 1) * HEAD_K],
                                   (((0,), (0,)), ((), ())), preferred_element_type=F32)


def _meta_state(k_m, v_m, z_m, wg2, bg):
    return pl.pallas_call(
        _meta_state_body,
        out_shape=jax.ShapeDtypeStruct((GLA_HEADS, HEAD_V, HEAD_K), F32),
        name="gla_meta_state",
    )(k_m, v_m, z_m, wg2, bg, _block_tri(N_META, False))


def _gla_body(*refs, reverse, cb, groups, has_prev):
    if has_prev:
        qk_ref, v_ref, z_ref, wg_ref, bg_ref, tri_ref, s0_ref, prev_ref, o_ref, st_ref = refs
    else:
        qk_ref, v_ref, z_ref, wg_ref, bg_ref, tri_ref, s0_ref, o_ref, st_ref = refs
        prev_ref = None

    @pl.when(pl.program_id(1) == 0)
    def _():
        st_ref[...] = s0_ref[...]

    nrow = cb * CHUNK
    row = lax.broadcasted_iota(jnp.int32, (nrow, nrow), 0)
    col = lax.broadcasted_iota(jnp.int32, (nrow, nrow), 1)
    shift = CHUNK.bit_length() - 1
    same_chunk = (row >> shift) == (col >> shift)
    msk = jnp.logical_and(same_chunk, (col > row) if reverse else (col <= row))
    scale = HEAD_K ** -0.5
    contract_last = (((1,), (1,)), ((), ()))
    contract_first = (((0,), (0,)), ((), ()))
    la_all = _gate_log_decay(z_ref[0], wg_ref[...], bg_ref[...])
    bc_all = _chunk_cumsum(tri_ref[...], la_all)
    chunks = [slice(c * CHUNK, (c + 1) * CHUNK) for c in range(cb)]
    order = range(cb - 1, -1, -1) if reverse else range(cb)
    for grp in (range(groups - 1, -1, -1) if reverse else range(groups)):
        g0 = grp * nrow
        bc = bc_all[g0:g0 + nrow]
        edge = [bc[c * CHUNK:c * CHUNK + 1] if reverse else bc[(c + 1) * CHUNK - 1:(c + 1) * CHUNK]
                for c in range(cb)]
        bl = jnp.concatenate([jnp.broadcast_to(e, (CHUNK, GLA_KEY)) for e in edge], axis=0)
        q = qk_ref[0, g0:g0 + nrow, 0:GLA_KEY] * scale
        k = qk_ref[0, g0:g0 + nrow, GLA_KEY:2 * GLA_KEY]
        qd = (q * jnp.exp(bc)).astype(BF16)
        kd = (k * jnp.exp(-bc)).astype(BF16)
        ke = (k * jnp.exp(bl - bc)).astype(BF16)
        dec = [jnp.exp(e) for e in edge]
        v = v_ref[0, g0:g0 + nrow, :].astype(BF16)
        for h in range(GLA_HEADS):
            ks = slice(h * HEAD_K, (h + 1) * HEAD_K)
            vs = slice(h * HEAD_V, (h + 1) * HEAD_V)
            sc = lax.dot_general(qd[:, ks], kd[:, ks], contract_last, preferred_element_type=F32)
            sc = jnp.where(msk, sc, 0.0).astype(BF16)
            o_intra = jnp.dot(sc, v[:, vs], preferred_element_type=F32)
            st = st_ref[h]
            for c in order:
                rows = chunks[c]
                out_rows = slice(g0 + c * CHUNK, g0 + (c + 1) * CHUNK)
                o = o_intra[rows] + lax.dot_general(qd[rows, ks], st.astype(BF16), contract_last,
                                                    preferred_element_type=F32)
                upd = lax.dot_general(v[rows, vs], ke[rows, ks], contract_first, preferred_element_type=F32)
                st = st * dec[c][:, ks] + upd
                if prev_ref is not None:
                    o = prev_ref[0, out_rows, vs] + o
                o_ref[0, out_rows, vs] = o
            st_ref[h] = st


def _gla(qk, v, z, wg2, bg, s0, prev, reverse):
    b, s, _ = qk.shape
    cb = 4
    groups = 2 if s % (2 * cb * CHUNK) == 0 else 1
    rows = groups * cb * CHUNK
    nb = s // rows
    btri = _block_tri(rows, reverse)
    if reverse:
        blk = lambda bi, i: (bi, nb - 1 - i, 0)
    else:
        blk = lambda bi, i: (bi, i, 0)
    in_specs = [pl.BlockSpec((1, rows, 2 * GLA_KEY), blk),
                pl.BlockSpec((1, rows, GLA_WIDTH), blk),
                pl.BlockSpec((1, rows, LANES), blk),
                _resident(wg2.shape), _resident(bg.shape), _resident(btri.shape), _resident(s0.shape)]
    args = [qk, v, z, wg2, bg, btri, s0]
    if prev is not None:
        in_specs.append(pl.BlockSpec((1, rows, GLA_WIDTH), blk))
        args.append(prev)
    return pl.pallas_call(
        functools.partial(_gla_body, reverse=reverse, cb=cb, groups=groups, has_prev=prev is not None),
        grid=(b, nb),
        in_specs=in_specs,
        out_specs=pl.BlockSpec((1, rows, GLA_WIDTH), blk),
        out_shape=jax.ShapeDtypeStruct((b, s, GLA_WIDTH), F32),
        scratch_shapes=[pltpu.VMEM((GLA_HEADS, HEAD_V, HEAD_K), F32)],
        compiler_params=_cparams(("parallel", "arbitrary")),
        name="gla_bwd" if reverse else "gla_fwd",
    )(*args)


def _dft_factors(length):
    best = None
    for n2 in range(8, length, 8):
        if length % n2 == 0:
            n1 = length // n2
            if best is None or n2 + 2 * n1 < best[1] + 2 * best[0]:
                best = (n1, n2)
    assert best is not None, length
    return best


def _dft_plan(length, channels):
    n1, n2 = _dft_factors(length)
    kc = _round_up(n1, 16)
    pitch = _round_up(n1, 8)
    if (pitch // 8) % 2 == 0:
        pitch += 8
    rows = _round_up(n2 * pitch + max(0, kc - pitch), 8)
    cw = LANES
    assert channels % cw == 0
    group = 2
    return n1, n2, kc, pitch, rows, cw, group


def _round_up(x, m):
    return (x + m - 1) // m * m


def _dft_tables(length, n1, n2, n1p):
    two_pi = 2.0 * math.pi
    r = np.arange(n2)
    pa = ((r[:, None] + N_META) * r[None, :]) % n2
    ang = two_pi * pa / n2
    a_stack = np.concatenate([np.cos(ang), -np.sin(ang)], axis=0)
    r1 = np.arange(n1)
    pb = (r1[:, None] * (r1[None, :] + N_META)) % n1
    angb = two_pi * pb / n1
    br = np.zeros((n1p, n1p)); bi = np.zeros((n1p, n1p))
    br[:n1, :n1] = np.cos(angb); bi[:n1, :n1] = -np.sin(angb)
    pt = ((r[:, None] + N_META) * (r1[None, :] + N_META)) % length
    angt = two_pi * pt / length
    twr = np.zeros((n2, 1, n1p)); twi = np.zeros((n2, 1, n1p))
    twr[:, 0, :n1] = np.cos(angt); twi[:, 0, :n1] = -np.sin(angt)
    return tuple(jnp.asarray(t, F32) for t in (a_stack, br, bi, twr, twi))


def _seqdft_body(x_ref, a_ref, br_ref, bi_ref, twr_ref, twi_ref, vr_any, vi_any,
                 ur_ref, ui_ref, vr_ref, vi_ref, sems, *, n1, n2, kc, pitch, cw, group):
    bidx = pl.program_id(0)
    cidx = pl.program_id(1)
    ncb = pl.num_programs(1)
    step = bidx * ncb + cidx
    last = pl.num_programs(0) * ncb - 1

    def writeback(bb, cc, start):
        col = pl.multiple_of(cc * cw, cw)
        for part, (src, dst) in enumerate(((vr_ref, vr_any), (vi_ref, vi_any))):
            cp = pltpu.make_async_copy(src, dst.at[bb, :, pl.ds(col, cw)], sems.at[part])
            if start:
                cp.start()
            else:
                cp.wait()

    @pl.when(step == 0)
    def _():
        ur_ref[...] = jnp.zeros(ur_ref.shape, F32)
        ui_ref[...] = jnp.zeros(ui_ref.shape, F32)

    a = a_ref[...].astype(BF16)

    def slab_group(m1, cnt):
        slabs = [x_ref[0, pl.ds(m1 + t, n2, stride=n1), :] for t in range(cnt)]
        rhs = (slabs[0] if cnt == 1 else jnp.concatenate(slabs, axis=1)).astype(BF16)
        p = jnp.dot(a, rhs, preferred_element_type=F32)
        for t in range(cnt):
            ur_ref[pl.ds(m1 + t, n2, stride=pitch), :] = p[:n2, t * cw:(t + 1) * cw]
            ui_ref[pl.ds(m1 + t, n2, stride=pitch), :] = p[n2:, t * cw:(t + 1) * cw]

    def step_a(it, carry):
        slab_group(it * group, group)
        return carry

    lax.fori_loop(0, n1 // group, step_a, 0, unroll=4)
    if n1 % group:
        slab_group((n1 // group) * group, n1 % group)

    br = br_ref[...]
    bi = bi_ref[...]

    @pl.when(step > 0)
    def _():
        prev_c = jnp.where(cidx == 0, ncb - 1, cidx - 1)
        prev_b = jnp.where(cidx == 0, bidx - 1, bidx)
        writeback(prev_b, prev_c, False)

    def step_c(j, carry):
        tr = twr_ref[j]
        ti = twi_ref[j]
        fm = jnp.concatenate([br * tr - bi * ti, br * ti + bi * tr], axis=0).astype(BF16)
        start = pl.multiple_of(j * pitch, 8)
        rhs = jnp.concatenate([ur_ref[pl.ds(start, kc), :], ui_ref[pl.ds(start, kc), :]],
                              axis=1).astype(BF16)
        r = jnp.dot(fm, rhs, preferred_element_type=F32)
        vr_ref[pl.ds(j, n1, stride=n2), :] = r[:n1, :cw] - r[kc:kc + n1, cw:]
        vi_ref[pl.ds(j, n1, stride=n2), :] = r[:n1, cw:] + r[kc:kc + n1, :cw]
        return carry

    lax.fori_loop(0, n2, step_c, 0, unroll=4)

    writeback(bidx, cidx, True)

    @pl.when(step == last)
    def _():
        writeback(bidx, cidx, False)


def _seqdft(four):
    b, length, c = four.shape
    n1, n2, kc, pitch, rows, cw, group = _dft_plan(length, c)
    tables = _dft_tables(length, n1, n2, kc)
    out = jax.ShapeDtypeStruct((b, length, c), F32)
    return pl.pallas_call(
        functools.partial(_seqdft_body, n1=n1, n2=n2, kc=kc, pitch=pitch, cw=cw, group=group),
        grid=(b, c // cw),
        in_specs=[pl.BlockSpec((1, length, cw), lambda bi, ci: (bi, 0, ci))]
                 + [_resident(t.shape) for t in tables],
        out_specs=[pl.BlockSpec(memory_space=pl.ANY), pl.BlockSpec(memory_space=pl.ANY)],
        out_shape=[out, out],
        scratch_shapes=[pltpu.VMEM((rows, cw), F32), pltpu.VMEM((rows, cw), F32),
                        pltpu.VMEM((length, cw), F32), pltpu.VMEM((length, cw), F32),
                        pltpu.SemaphoreType.DMA((2,))],
        compiler_params=_cparams(("arbitrary", "arbitrary")),
        name="seqdft",
    )(four, *tables)


def _outproj_body(o_ref, g_ref, vr_ref, vi_ref, x_ref, gw_ref, cw_ref, sw_ref, wo_ref, n2w_ref, wr_ref, br_ref,
                  base_in_ref, h_ref, hn_ref, route_ref, cnt_ref, base_ref, *, fnorm):
    first = jnp.logical_and(pl.program_id(0) == 0, pl.program_id(1) == 0)

    @pl.when(first)
    def _():
        base_ref[...] = base_in_ref[...]

    tm = o_ref.shape[1]
    parts = []
    for h in range(GLA_HEADS):
        vs = slice(h * HEAD_V, (h + 1) * HEAD_V)
        o = o_ref[0, :, vs]
        o = o * lax.rsqrt(jnp.mean(o * o, axis=-1, keepdims=True) + EPS)
        o = o * gw_ref[:, vs]
        g = g_ref[0, :, vs]
        parts.append((o * (g * (1.0 / (1.0 + jnp.exp(-g))))).astype(BF16))
    cw = cw_ref[...].astype(BF16)
    sw = sw_ref[...].astype(BF16)
    for gi in range(FNET_WIDTH // FNET_GROUP_W):
        cs = slice(gi * FNET_GROUP_W, (gi + 1) * FNET_GROUP_W)
        vr = vr_ref[0, :, cs].astype(BF16)
        vi = vi_ref[0, :, cs].astype(BF16)
        f = (jnp.dot(vr, cw, preferred_element_type=F32) + jnp.dot(vi, sw, preferred_element_type=F32))
        parts.append((f * fnorm).astype(BF16))
    merged = jnp.concatenate(parts, axis=-1)
    h1 = x_ref[0] + jnp.dot(merged, wo_ref[...], preferred_element_type=F32)
    h_ref[0] = h1
    hn = h1 * lax.rsqrt(jnp.mean(h1 * h1, axis=-1, keepdims=True) + EPS) * n2w_ref[...]
    hn_ref[0] = hn

    hn_hi = hn.astype(BF16)
    hn_lo = (hn - hn_hi.astype(F32)).astype(BF16)
    pp = jnp.dot(jnp.concatenate([hn_hi, hn_lo], axis=0), wr_ref[...], preferred_element_type=F32)
    logits = ((pp[:tm, :LANES] + pp[:tm, LANES:]) + (pp[tm:, :LANES] + pp[tm:, LANES:])) + br_ref[...]
    lane = lax.broadcasted_iota(jnp.int32, (tm, LANES), 1)
    neg = jnp.float32(-jnp.inf)
    gl = jnp.where(lane < N_GROUPS, logits, neg)
    gmax = jnp.max(gl, axis=-1, keepdims=True)
    grp = jnp.min(jnp.where(gl == gmax, lane, LANES), axis=-1, keepdims=True)
    grp_w = 1.0 / jnp.sum(jnp.exp(gl - gmax), axis=-1, keepdims=True)
    elane = lane - N_GROUPS
    member = jnp.logical_and(jnp.logical_and(elane >= 0, elane < N_EXPERTS),
                             (elane >> 3) == grp)
    el = jnp.where(member, logits, neg)
    m0 = jnp.max(el, axis=-1, keepdims=True)
    i0 = jnp.min(jnp.where(el == m0, lane, LANES), axis=-1, keepdims=True)
    el2 = jnp.where(lane == i0, neg, el)
    m1 = jnp.max(el2, axis=-1, keepdims=True)
    i1 = jnp.min(jnp.where(el2 == m1, lane, LANES), axis=-1, keepdims=True)
    p1 = jnp.exp(m1 - m0)
    den = 1.0 + p1
    w0 = (1.0 / den) * grp_w
    w1 = (p1 / den) * grp_w
    e0 = i0 - N_GROUPS
    e1 = i1 - N_GROUPS

    oh0 = (lane == e0).astype(BF16)
    oh1 = (lane == e1).astype(BF16)
    r_i = lax.broadcasted_iota(jnp.int32, (tm, tm), 0)
    c_i = lax.broadcasted_iota(jnp.int32, (tm, tm), 1)
    ltri = (c_i < r_i).astype(BF16)
    c0 = jnp.dot(ltri, oh0, preferred_element_type=F32)
    c1 = jnp.dot(ltri, oh1, preferred_element_type=F32)
    oh0f = oh0.astype(F32)
    oh1f = oh1.astype(F32)
    tot0 = jnp.sum(oh0f, axis=0, keepdims=True)
    tot1 = jnp.sum(oh1f, axis=0, keepdims=True)
    base = base_ref[...]
    rank0 = jnp.sum(oh0f * (c0 + base), axis=-1, keepdims=True)
    rank1 = jnp.sum(oh1f * (c1 + base + tot0), axis=-1, keepdims=True)
    new_base = base + tot0 + tot1
    base_ref[...] = new_base
    cnt_ref[...] = new_base

    route = jnp.where(lane == 0, e0.astype(F32),
            jnp.where(lane == 1, e1.astype(F32),
            jnp.where(lane == 2, w0,
            jnp.where(lane == 3, w1,
            jnp.where(lane == 4, rank0,
            jnp.where(lane == 5, rank1, 0.0))))))
    route_ref[0] = route


def _outproj(o, g, vr, vi, x, gw, cw, sw, wo, n2w, wr, br, base_in, length):
    b, s, d = x.shape
    tm = _row_tile(s, 512)
    fnorm = 1.0 / math.sqrt(length * FNET_GROUP_W)
    row = lambda w: pl.BlockSpec((1, tm, w), lambda bi, i: (bi, i, 0))
    return pl.pallas_call(
        functools.partial(_outproj_body, fnorm=fnorm),
        grid=(b, s // tm),
        in_specs=[row(GLA_WIDTH), row(GLA_WIDTH), row(FNET_WIDTH), row(FNET_WIDTH), row(d),
                  _resident(gw.shape), _resident(cw.shape), _resident(sw.shape), _resident(wo.shape),
                  _resident(n2w.shape), _resident(wr.shape), _resident(br.shape), _resident(base_in.shape)],
        out_specs=[row(d), row(d), row(LANES), pl.BlockSpec((1, LANES), lambda bi, i: (0, 0))],
        out_shape=[jax.ShapeDtypeStruct((b, s, d), F32), jax.ShapeDtypeStruct((b, s, d), F32),
                   jax.ShapeDtypeStruct((b, s, LANES), F32), jax.ShapeDtypeStruct((1, LANES), F32)],
        scratch_shapes=[pltpu.VMEM((1, LANES), F32)],
        compiler_params=_cparams(("arbitrary", "arbitrary")),
        name="outproj_router",
    )(o, g, vr, vi, x, gw, cw, sw, wo, n2w, wr, br, base_in)


DISPATCH_WINDOW = 256

def _dispatch_body(slot_ref, zblk_ref, zval_ref, *refs, counts, win):
    srcs = refs[:len(counts)]
    xs_out, zero_ref, tile_ref, zsem, lsems, ssems = refs[len(counts):]
    step = pl.program_id(0)
    nsteps = sum(counts) // win

    @pl.when(step == 0)
    def _():
        zero_ref[...] = jnp.zeros(zero_ref.shape, zero_ref.dtype)

        def zcopy(i):
            row = pl.multiple_of(zblk_ref[i] * MOE_BM, MOE_BM)
            return pltpu.make_async_copy(zero_ref, xs_out.at[pl.ds(row, MOE_BM)], zsem)

        for i in range(2 * N_EXPERTS):
            @pl.when(zval_ref[i] != 0)
            def _():
                zcopy(i).start()
        for i in range(2 * N_EXPERTS):
            @pl.when(zval_ref[i] != 0)
            def _():
                zcopy(i).wait()

    def load(w, start):
        lo = 0
        for src, n in zip(srcs, counts):
            hi = lo + n // win

            @pl.when(jnp.logical_and(w >= lo, w < hi))
            def _(src=src, lo=lo):
                row0 = pl.multiple_of((w - lo) * win, 8)
                b = lax.rem(w, 3)
                cp = pltpu.make_async_copy(src.at[pl.ds(row0, win)], tile_ref.at[b], lsems.at[b])
                if start:
                    cp.start()
                else:
                    cp.wait()

            lo = hi

    def scatter(w, start):
        b = lax.rem(w, 3)
        sem = ssems.at[lax.rem(w, 2)]
        tok0 = w * win

        def body(r, c):
            for k in range(TOP_K):
                s = slot_ref[2 * (tok0 + r) + k]
                cp = pltpu.make_async_copy(tile_ref.at[b, pl.ds(r, 1)], xs_out.at[pl.ds(s, 1)], sem)
                if start:
                    cp.start()
                else:
                    cp.wait()
            return c

        lax.fori_loop(0, win, body, 0, unroll=8)

    @pl.when(step == 0)
    def _():
        load(step, True)

    @pl.when(step + 1 < nsteps)
    def _():
        load(step + 1, True)

    load(step, False)
    scatter(step, True)

    @pl.when(step > 0)
    def _():
        scatter(step - 1, False)

    @pl.when(step == nsteps - 1)
    def _():
        scatter(step, False)


def _dispatch(slots, zblk, zval, sources, n_rows):
    d = sources[0].shape[1]
    dtype = sources[0].dtype
    counts = tuple(s.shape[0] for s in sources)
    win = functools.reduce(math.gcd, counts + (DISPATCH_WINDOW,))
    return pl.pallas_call(
        functools.partial(_dispatch_body, counts=counts, win=win),
        grid_spec=pltpu.PrefetchScalarGridSpec(
            num_scalar_prefetch=3,
            grid=(sum(counts) // win,),
            in_specs=[pl.BlockSpec(memory_space=pl.ANY) for _ in sources],
            out_specs=pl.BlockSpec(memory_space=pl.ANY),
            scratch_shapes=[pltpu.VMEM((MOE_BM, d), dtype), pltpu.VMEM((3, win, d), dtype),
                            pltpu.SemaphoreType.DMA(()), pltpu.SemaphoreType.DMA((3,)),
                            pltpu.SemaphoreType.DMA((2,))]),
        out_shape=jax.ShapeDtypeStruct((n_rows, d), dtype),
        compiler_params=_cparams(("arbitrary",)),
        name="moe_dispatch",
    )(slots, zblk, zval, *sources)


def _experts_body(be_ref, nu_ref, x_ref, wg_ref, wu_ref, wd_ref, y_ref, wgb_ref, wub_ref, wdb_ref):
    i = pl.program_id(0)
    used = i < nu_ref[0]
    ic = jnp.minimum(i, nu_ref[0] - 1)
    fresh = jnp.logical_or(i == 0, be_ref[ic] != be_ref[jnp.maximum(ic - 1, 0)])

    @pl.when(jnp.logical_and(used, fresh))
    def _():
        wgb_ref[...] = wg_ref[0].astype(BF16)
        wub_ref[...] = wu_ref[0].astype(BF16)
        wdb_ref[...] = wd_ref[0].astype(BF16)

    @pl.when(used)
    def _():
        xb = x_ref[...].astype(BF16)
        gate = jnp.dot(xb, wgb_ref[...], preferred_element_type=F32)
        up = jnp.dot(xb, wub_ref[...], preferred_element_type=F32)
        hid = (gate * (1.0 / (1.0 + jnp.exp(-gate))) * up).astype(BF16)
        y_ref[...] = jnp.dot(hid, wdb_ref[...], preferred_element_type=F32)

    @pl.when(jnp.logical_not(used))
    def _():
        y_ref[...] = jnp.zeros(y_ref.shape, F32)


def _experts(block_expert, n_used, xs, wg, wu, wd):
    p, d = xs.shape
    nb = p // MOE_BM
    blk = lambda i, be, nu: (jnp.minimum(i, nu[0] - 1), 0)
    wsel = lambda i, be, nu: (be[jnp.minimum(i, nu[0] - 1)], 0, 0)
    return pl.pallas_call(
        _experts_body,
        grid_spec=pltpu.PrefetchScalarGridSpec(
            num_scalar_prefetch=2,
            grid=(nb,),
            in_specs=[pl.BlockSpec((MOE_BM, d), blk),
                      pl.BlockSpec((1, d, EXPERT_FF), wsel),
                      pl.BlockSpec((1, d, EXPERT_FF), wsel),
                      pl.BlockSpec((1, EXPERT_FF, d), wsel)],
            out_specs=pl.BlockSpec((MOE_BM, d), lambda i, be, nu: (i, 0)),
            scratch_shapes=[pltpu.VMEM((d, EXPERT_FF), BF16), pltpu.VMEM((d, EXPERT_FF), BF16),
                            pltpu.VMEM((EXPERT_FF, d), BF16)]),
        out_shape=jax.ShapeDtypeStruct((p, d), F32),
        compiler_params=_cparams(("arbitrary",)),
        name="moe_experts",
    )(block_expert, n_used, xs, wg, wu, wd)


def _combine_body(slot_ref, h_ref, route_ref, ys_any, fw_ref, o_ref, ybuf_ref, sems, *, nt):
    tm = h_ref.shape[0]
    i = pl.program_id(0)

    def copy(tile, r, k):
        buf = lax.rem(tile, 2)
        s = slot_ref[2 * (tile * tm + r) + k]
        return pltpu.make_async_copy(ys_any.at[pl.ds(s, 1)], ybuf_ref.at[buf, k, pl.ds(r, 1)], sems.at[buf])

    def issue_tile(tile):
        def body(r, carry):
            copy(tile, r, 0).start()
            copy(tile, r, 1).start()
            return carry
        lax.fori_loop(0, tm, body, 0, unroll=8)

    @pl.when(i == 0)
    def _():
        issue_tile(i)

    @pl.when(i + 1 < nt)
    def _():
        issue_tile(i + 1)

    def drain(r, carry):
        copy(i, r, 0).wait()
        copy(i, r, 1).wait()
        return carry

    lax.fori_loop(0, tm, drain, 0, unroll=8)

    buf = lax.rem(i, 2)
    w0 = route_ref[:, 2:3]
    w1 = route_ref[:, 3:4]
    h = h_ref[...] + (ybuf_ref[buf, 0] * w0 + ybuf_ref[buf, 1] * w1)
    o_ref[...] = h * lax.rsqrt(jnp.mean(h * h, axis=-1, keepdims=True) + EPS) * fw_ref[...]


def _combine(slots, h2d, route2d, ys, fw):
    t, d = h2d.shape
    tm = _row_tile(t, 256)
    nt = t // tm
    return pl.pallas_call(
        functools.partial(_combine_body, nt=nt),
        grid_spec=pltpu.PrefetchScalarGridSpec(
            num_scalar_prefetch=1,
            grid=(nt,),
            in_specs=[pl.BlockSpec((tm, d), lambda i, sl: (i, 0)),
                      pl.BlockSpec((tm, LANES), lambda i, sl: (i, 0)),
                      pl.BlockSpec(memory_space=pl.ANY),
                      pl.BlockSpec((1, d), lambda i, sl: (0, 0))],
            out_specs=pl.BlockSpec((tm, d), lambda i, sl: (i, 0)),
            scratch_shapes=[pltpu.VMEM((2, TOP_K, tm, d), F32), pltpu.SemaphoreType.DMA((2,))]),
        out_shape=jax.ShapeDtypeStruct((t, d), F32),
        compiler_params=_cparams(("arbitrary",)),
        name="moe_combine",
    )(slots, h2d, route2d, ys, fw)


def _prep_weights(norm1_w, w_in, w_gate_up, b_gate_up, gla_norm_w, w_out, norm2_w, w_router_group,
                  b_router_group, w_router_expert, b_router_expert, w_expert_gate, w_expert_up,
                  w_expert_down, final_norm_w):
    w = w_in[0]
    wz = jnp.pad(w[:, OFF_ZF:OFF_F], ((0, 0), (0, LANES - 2 * GATE_RANK)))
    wg_f = jnp.pad(w_gate_up[0, 0], ((0, LANES - GATE_RANK), (0, 0)))
    wg_b = jnp.pad(w_gate_up[0, 1], ((GATE_RANK, LANES - 2 * GATE_RANK), (0, 0)))
    rpad = LANES - N_GROUPS - N_EXPERTS
    wr = jnp.pad(jnp.concatenate([w_router_group[0], w_router_expert[0]], axis=1), ((0, 0), (0, rpad)))
    br = jnp.pad(jnp.concatenate([b_router_group[0], b_router_expert[0]]), (0, rpad))[None]
    k = np.arange(FNET_GROUP_W)
    ang = 2.0 * math.pi * ((k[:, None] * k[None, :]) % FNET_GROUP_W) / FNET_GROUP_W
    return dict(
        nw1=norm1_w[0][None], wqk=w[:, :OFF_V].astype(BF16), wv=w[:, OFF_V:OFF_G].astype(BF16), wg=w[:, OFF_G:OFF_ZF].astype(BF16),
        wz=wz.astype(BF16), wf=w[:, OFF_F:].astype(BF16),
        wg_f=_split_bf16(wg_f), wg_b=_split_bf16(wg_b), bg_f=b_gate_up[0, 0][None], bg_b=b_gate_up[0, 1][None],
        gw=gla_norm_w[0][None], cw=jnp.asarray(np.cos(ang), F32), sw=jnp.asarray(np.sin(ang), F32),
        wo=w_out[0].astype(BF16), nw2=norm2_w[0][None], wr=_split_bf16(wr), br=br,
        weg=w_expert_gate[0], weu=w_expert_up[0], wed=w_expert_down[0],
        fw=final_norm_w[None])


def _split_bf16(w):
    hi = w.astype(BF16)
    lo = (w - hi.astype(F32)).astype(BF16)
    return jnp.concatenate([hi, lo], axis=1)


def _mixer_and_route(x, meta, pw, base_in):
    b, s, d = x.shape
    length = s + N_META
    qk, v, g, z, four = _inproj(x, meta["four"], pw["nw1"], pw["wqk"], pw["wv"], pw["wg"], pw["wz"], pw["wf"])
    s0 = _meta_state(meta["k"], meta["v"], meta["z"], pw["wg_f"], pw["bg_f"])
    o_f = _gla(qk, v, z, pw["wg_f"], pw["bg_f"], s0, None, reverse=False)
    o = _gla(qk, v, z, pw["wg_b"], pw["bg_b"], jnp.zeros_like(s0), o_f, reverse=True)
    vr, vi = _seqdft(four)
    return _outproj(o, g, vr, vi, x, pw["gw"], pw["cw"], pw["sw"], pw["wo"], pw["nw2"], pw["wr"], pw["br"],
                    base_in, length)


def kernel(x_prompt, x_sample, meta_tokens, norm1_w, w_in, w_gate_up, b_gate_up, gla_norm_w, w_out, norm2_w,
           w_router_group, b_router_group, w_router_expert, b_router_expert, w_expert_gate, w_expert_up,
           w_expert_down, final_norm_w):
    pw = _prep_weights(norm1_w, w_in, w_gate_up, b_gate_up, gla_norm_w, w_out, norm2_w, w_router_group,
                       b_router_group, w_router_expert, b_router_expert, w_expert_gate, w_expert_up,
                       w_expert_down, final_norm_w)
    qk_m, v_m, _, z_m, four_m = _inproj(meta_tokens[None], None, pw["nw1"], pw["wqk"], pw["wv"], pw["wg"],
                                         pw["wz"], pw["wf"])
    meta = dict(k=qk_m[0, :, GLA_KEY:], v=v_m[0], z=z_m[0], four=four_m)

    xs_in = (x_prompt, x_sample)
    base = jnp.zeros((1, LANES), F32)
    mixed = []
    for x in xs_in:
        h, hn, route, base = _mixer_and_route(x, meta, pw, base)
        mixed.append((h, hn, route))

    counts = base[0, :N_EXPERTS].astype(jnp.int32)
    padded = (counts + MOE_BM - 1) // MOE_BM * MOE_BM
    pends = jnp.cumsum(padded)
    pstart = (pends - padded).astype(F32)
    n_assign = sum(x.shape[0] * x.shape[1] for x in xs_in) * TOP_K
    nb = -(-(n_assign + N_EXPERTS * (MOE_BM - 1)) // MOE_BM)
    n_used = (pends[-1:] // MOE_BM).astype(jnp.int32)
    blk_row = jnp.arange(nb, dtype=jnp.int32) * MOE_BM
    block_expert = jnp.minimum(jnp.sum((pends[None, :] <= blk_row[:, None]).astype(jnp.int32), axis=1),
                               N_EXPERTS - 1)
    tail = n_used[0] + jnp.arange(N_EXPERTS, dtype=jnp.int32)
    zblk = jnp.concatenate([jnp.maximum(pends // MOE_BM - 1, 0), jnp.minimum(tail, nb - 1)]).astype(jnp.int32)
    zval = jnp.concatenate([padded > 0, tail < nb]).astype(jnp.int32)

    slot_list = []
    for (h, hn, route) in mixed:
        t = h.shape[0] * h.shape[1]
        r2 = route.reshape(t, LANES)
        e = r2[:, 0:2].astype(jnp.int32)
        onehot = (e[:, :, None] == jnp.arange(N_EXPERTS, dtype=jnp.int32)).astype(F32)
        slots = (jnp.sum(onehot * pstart, axis=-1) + r2[:, 4:6]).astype(jnp.int32).reshape(-1)
        slot_list.append(slots)
    xs = _dispatch(jnp.concatenate(slot_list), zblk, zval,
                   [hn.reshape(-1, hn.shape[-1]) for (_, hn, _) in mixed], nb * MOE_BM)

    ys = _experts(block_expert, n_used, xs, pw["weg"], pw["weu"], pw["wed"])

    outs = []
    for x, (h, hn, route), slots in zip(xs_in, mixed, slot_list):
        t = h.shape[0] * h.shape[1]
        y = _combine(slots, h.reshape(t, D_MODEL), route.reshape(t, LANES), ys, pw["fw"])
        outs.append(y.reshape(x.shape))
    return tuple(outs)
```

```python
import functools
import math

import numpy as np
import jax
import jax.numpy as jnp
from jax import lax
from jax.experimental import pallas as pl
from jax.experimental.pallas import tpu as pltpu

F32 = jnp.float32
BF16 = jnp.bfloat16

D_MODEL = 2048
N_META = 16
GLA_WIDTH = 1024
FNET_WIDTH = 1024
GLA_HEADS = 4
HEAD_V = 256
GLA_KEY = 512
HEAD_K = 128
GATE_RANK = 16
GATE_TAU = 16.0
CHUNK = 64
FNET_GROUP_W = 256
N_GROUPS = 4
EXPERTS_PER_GROUP = 8
N_EXPERTS = 32
TOP_K = 2
EXPERT_FF = 512
EPS = 1e-6
OFF_K = 512
OFF_V = 1024
OFF_G = 2048
OFF_ZF = 3072
OFF_F = 3104

LANES = 128
MOE_BM = 512
VMEM_LIMIT = 56 * 1024 * 1024


def _cparams(semantics, vmem=VMEM_LIMIT):
    return pltpu.CompilerParams(dimension_semantics=semantics, vmem_limit_bytes=vmem)


def _resident(shape):
    nd = len(shape)
    return pl.BlockSpec(shape, lambda *_: (0,) * nd, pipeline_mode=pl.Buffered(1))


def _row_tile(n, target):
    t = min(n, target)
    while n % t:
        t -= 8
    return t


def _inproj_body(x_ref, fm_ref, nw_ref, wqk_ref, wv_ref, wg_ref, wz_ref, wf_ref,
                 qk_ref, v_ref, g_ref, z_ref, f_ref, *, nt):
    i = pl.program_id(1)

    @pl.when(i < nt)
    def _():
        x = x_ref[0]
        y = x * lax.rsqrt(jnp.mean(x * x, axis=-1, keepdims=True) + EPS)
        yb = (y * nw_ref[...]).astype(BF16)
        qk_ref[0] = jnp.dot(yb, wqk_ref[...], preferred_element_type=F32)
        v_ref[0] = jnp.dot(yb, wv_ref[...], preferred_element_type=F32).astype(BF16)
        g_ref[0] = jnp.dot(yb, wg_ref[...], preferred_element_type=F32)
        z_ref[0] = jnp.dot(yb, wz_ref[...], preferred_element_type=F32)
        f_ref[0] = jnp.dot(yb, wf_ref[...], preferred_element_type=F32)

    @pl.when(i == nt)
    def _():
        f_ref[0, 0:N_META, :] = fm_ref[0]


def _inproj(x, four_meta, nw, wqk, wv, wg, wz, wf):
    b, s, d = x.shape
    tm = _row_tile(s, 512)
    nt = s // tm
    extra = 0 if four_meta is None else 1
    if four_meta is None:
        four_meta = jnp.zeros((1, N_META, FNET_WIDTH), F32)
    row = lambda w: pl.BlockSpec((1, tm, w), lambda bi, i: (bi, jnp.minimum(i, nt - 1), 0))
    return pl.pallas_call(
        functools.partial(_inproj_body, nt=nt),
        grid=(b, nt + extra),
        in_specs=[row(d), _resident(four_meta.shape), _resident((1, d)), _resident(wqk.shape),
                  _resident(wv.shape), _resident(wg.shape), _resident(wz.shape), _resident(wf.shape)],
        out_specs=[row(2 * GLA_KEY), row(GLA_WIDTH), row(GLA_WIDTH), row(LANES),
                   pl.BlockSpec((1, tm, FNET_WIDTH), lambda bi, i: (bi, i, 0))],
        out_shape=[jax.ShapeDtypeStruct((b, s, 2 * GLA_KEY), F32),
                   jax.ShapeDtypeStruct((b, s, GLA_WIDTH), BF16),
                   jax.ShapeDtypeStruct((b, s, GLA_WIDTH), F32),
                   jax.ShapeDtypeStruct((b, s, LANES), F32),
                   jax.ShapeDtypeStruct((b, s + extra * N_META, FNET_WIDTH), F32)],
        compiler_params=_cparams(("parallel", "arbitrary")),
        name="inproj",
    )(x, four_meta, nw, wqk, wv, wg, wz, wf)


def _log_sigmoid(x):
    return jnp.minimum(x, 0.0) - jnp.log1p(jnp.exp(-jnp.abs(x)))


def _gate_log_decay(z, wg2, bg):
    n = z.shape[0]
    z_hi = z.astype(BF16)
    z_lo = (z - z_hi.astype(F32)).astype(BF16)
    pp = jnp.dot(jnp.concatenate([z_hi, z_lo], axis=0), wg2, preferred_element_type=F32)
    pre = ((pp[:n, :GLA_KEY] + pp[:n, GLA_KEY:]) + (pp[n:, :GLA_KEY] + pp[n:, GLA_KEY:])) + bg
    return _log_sigmoid(pre) * (1.0 / GATE_TAU)


def _chunk_cumsum(btri, la):
    la_hi = la.astype(BF16)
    la_lo = (la - la_hi.astype(F32)).astype(BF16)
    pp = jnp.dot(btri, jnp.concatenate([la_hi, la_lo], axis=1), preferred_element_type=F32)
    return pp[:, :GLA_KEY] + pp[:, GLA_KEY:]


def _block_tri(rows, reverse):
    r = np.arange(rows)
    same = (r[:, None] // CHUNK) == (r[None, :] // CHUNK)
    tri = (r[None, :] >= r[:, None]) if reverse else (r[None, :] <= r[:, None])
    return jnp.asarray(same & tri, BF16)


def _meta_state_body(k_ref, v_ref, z_ref, wg_ref, bg_ref, tri_ref, s_ref):
    la = _gate_log_decay(z_ref[...], wg_ref[...], bg_ref[...])
    bc = _chunk_cumsum(tri_ref[...], la)
    bl = bc[N_META - 1:N_META]
    ke = (k_ref[...] * jnp.exp(bl - bc)).astype(BF16)
    v = v_ref[...].astype(BF16)
    for h in range(GLA_HEADS):
        s_ref[h] = lax.dot_general(v[:, h * HEAD_V:(h + 1) * HEAD_V], ke[:, h * HEAD_K:(h + 1) * HEAD_K],
                                   (((0,), (0,)), ((), ())), preferred_element_type=F32)


def _meta_state(k_m, v_m, z_m, wg2, bg):
    return pl.pallas_call(
        _meta_state_body,
        out_shape=jax.ShapeDtypeStruct((GLA_HEADS, HEAD_V, HEAD_K), F32),
        name="gla_meta_state",
    )(k_m, v_m, z_m, wg2, bg, _block_tri(N_META, False))


def _gla_body(*refs, reverse, cb, groups, has_prev):
    if has_prev:
        qk_ref, v_ref, z_ref, wg_ref, bg_ref, tri_ref, s0_ref, prev_ref, o_ref, st_ref = refs
    else:
        qk_ref, v_ref, z_ref, wg_ref, bg_ref, tri_ref, s0_ref, o_ref, st_ref = refs
        prev_ref = None

    @pl.when(pl.program_id(1) == 0)
    def _():
        st_ref[...] = s0_ref[...]

    nrow = cb * CHUNK
    row = lax.broadcasted_iota(jnp.int32, (nrow, nrow), 0)
    col = lax.broadcasted_iota(jnp.int32, (nrow, nrow), 1)
    shift = CHUNK.bit_length() - 1
    same_chunk = (row >> shift) == (col >> shift)
    msk = jnp.logical_and(same_chunk, (col > row) if reverse else (col <= row))
    scale = HEAD_K ** -0.5
    contract_last = (((1,), (1,)), ((), ()))
    contract_first = (((0,), (0,)), ((), ()))
    la_all = _gate_log_decay(z_ref[0], wg_ref[...], bg_ref[...])
    bc_all = _chunk_cumsum(tri_ref[...], la_all)
    chunks = [slice(c * CHUNK, (c + 1) * CHUNK) for c in range(cb)]
    order = range(cb - 1, -1, -1) if reverse else range(cb)
    for grp in (range(groups - 1, -1, -1) if reverse else range(groups)):
        g0 = grp * nrow
        bc = bc_all[g0:g0 + nrow]
        edge = [bc[c * CHUNK:c * CHUNK + 1] if reverse else bc[(c + 1) * CHUNK - 1:(c + 1) * CHUNK]
                for c in range(cb)]
        bl = jnp.concatenate([jnp.broadcast_to(e, (CHUNK, GLA_KEY)) for e in edge], axis=0)
        q = qk_ref[0, g0:g0 + nrow, 0:GLA_KEY] * scale
        k = qk_ref[0, g0:g0 + nrow, GLA_KEY:2 * GLA_KEY]
        qd = (q * jnp.exp(bc)).astype(BF16)
        kd = (k * jnp.exp(-bc)).astype(BF16)
        ke = (k * jnp.exp(bl - bc)).astype(BF16)
        dec = [jnp.exp(e) for e in edge]
        v = v_ref[0, g0:g0 + nrow, :].astype(BF16)
        for h in range(GLA_HEADS):
            ks = slice(h * HEAD_K, (h + 1) * HEAD_K)
            vs = slice(h * HEAD_V, (h + 1) * HEAD_V)
            sc = lax.dot_general(qd[:, ks], kd[:, ks], contract_last, preferred_element_type=F32)
            sc = jnp.where(msk, sc, 0.0).astype(BF16)
            o_intra = jnp.dot(sc, v[:, vs], preferred_element_type=F32)
            st = st_ref[h]
            for c in order:
                rows = chunks[c]
                out_rows = slice(g0 + c * CHUNK, g0 + (c + 1) * CHUNK)
                o = o_intra[rows] + lax.dot_general(qd[rows, ks], st.astype(BF16), contract_last,
                                                    preferred_element_type=F32)
                upd = lax.dot_general(v[rows, vs], ke[rows, ks], contract_first, preferred_element_type=F32)
                st = st * dec[c][:, ks] + upd
                if prev_ref is not None:
                    o = prev_ref[0, out_rows, vs] + o
                o_ref[0, out_rows, vs] = o
            st_ref[h] = st


def _gla(qk, v, z, wg2, bg, s0, prev, reverse):
    b, s, _ = qk.shape
    cb = 4
    groups = 2 if s % (2 * cb * CHUNK) == 0 else 1
    rows = groups * cb * CHUNK
    nb = s // rows
    btri = _block_tri(rows, reverse)
    if reverse:
        blk = lambda bi, i: (bi, nb - 1 - i, 0)
    else:
        blk = lambda bi, i: (bi, i, 0)
    in_specs = [pl.BlockSpec((1, rows, 2 * GLA_KEY), blk),
                pl.BlockSpec((1, rows, GLA_WIDTH), blk),
                pl.BlockSpec((1, rows, LANES), blk),
                _resident(wg2.shape), _resident(bg.shape), _resident(btri.shape), _resident(s0.shape)]
    args = [qk, v, z, wg2, bg, btri, s0]
    if prev is not None:
        in_specs.append(pl.BlockSpec((1, rows, GLA_WIDTH), blk))
        args.append(prev)
    return pl.pallas_call(
        functools.partial(_gla_body, reverse=reverse, cb=cb, groups=groups, has_prev=prev is not None),
        grid=(b, nb),
        in_specs=in_specs,
        out_specs=pl.BlockSpec((1, rows, GLA_WIDTH), blk),
        out_shape=jax.ShapeDtypeStruct((b, s, GLA_WIDTH), F32),
        scratch_shapes=[pltpu.VMEM((GLA_HEADS, HEAD_V, HEAD_K), F32)],
        compiler_params=_cparams(("parallel", "arbitrary")),
        name="gla_bwd" if reverse else "gla_fwd",
    )(*args)


def _dft_factors(length):
    best = None
    for n2 in range(8, length, 8):
        if length % n2 == 0:
            n1 = length // n2
            if best is None or n2 + 2 * n1 < best[1] + 2 * best[0]:
                best = (n1, n2)
    assert best is not None, length
    return best


def _dft_plan(length, channels):
    n1, n2 = _dft_factors(length)
    kc = _round_up(n1, 16)
    pitch = _round_up(n1, 8)
    if (pitch // 8) % 2 == 0:
        pitch += 8
    rows = _round_up(n2 * pitch + max(0, kc - pitch), 8)
    cw = LANES
    assert channels % cw == 0
    group = 2
    return n1, n2, kc, pitch, rows, cw, group


def _round_up(x, m):
    return (x + m - 1) // m * m


def _dft_tables(length, n1, n2, n1p):
    two_pi = 2.0 * math.pi
    r = np.arange(n2)
    pa = ((r[:, None] + N_META) * r[None, :]) % n2
    ang = two_pi * pa / n2
    a_stack = np.concatenate([np.cos(ang), -np.sin(ang)], axis=0)
    r1 = np.arange(n1)
    pb = (r1[:, None] * (r1[None, :] + N_META)) % n1
    angb = two_pi * pb / n1
    br = np.zeros((n1p, n1p)); bi = np.zeros((n1p, n1p))
    br[:n1, :n1] = np.cos(angb); bi[:n1, :n1] = -np.sin(angb)
    pt = ((r[:, None] + N_META) * (r1[None, :] + N_META)) % length
    angt = two_pi * pt / length
    twr = np.zeros((n2, 1, n1p)); twi = np.zeros((n2, 1, n1p))
    twr[:, 0, :n1] = np.cos(angt); twi[:, 0, :n1] = -np.sin(angt)
    return tuple(jnp.asarray(t, F32) for t in (a_stack, br, bi, twr, twi))


def _seqdft_body(x_ref, a_ref, br_ref, bi_ref, twr_ref, twi_ref, vr_any, vi_any,
                 ur_ref, ui_ref, vr_ref, vi_ref, sems, *, n1, n2, kc, pitch, cw, group):
    bidx = pl.program_id(0)
    cidx = pl.program_id(1)
    ncb = pl.num_programs(1)
    step = bidx * ncb + cidx
    last = pl.num_programs(0) * ncb - 1

    def writeback(bb, cc, start):
        col = pl.multiple_of(cc * cw, cw)
        for part, (src, dst) in enumerate(((vr_ref, vr_any), (vi_ref, vi_any))):
            cp = pltpu.make_async_copy(src, dst.at[bb, :, pl.ds(col, cw)], sems.at[part])
            if start:
                cp.start()
            else:
                cp.wait()

    @pl.when(step == 0)
    def _():
        ur_ref[...] = jnp.zeros(ur_ref.shape, F32)
        ui_ref[...] = jnp.zeros(ui_ref.shape, F32)

    a = a_ref[...].astype(BF16)

    def slab_group(m1, cnt):
        slabs = [x_ref[0, pl.ds(m1 + t, n2, stride=n1), :] for t in range(cnt)]
        rhs = (slabs[0] if cnt == 1 else jnp.concatenate(slabs, axis=1)).astype(BF16)
        p = jnp.dot(a, rhs, preferred_element_type=F32)
        for t in range(cnt):
            ur_ref[pl.ds(m1 + t, n2, stride=pitch), :] = p[:n2, t * cw:(t + 1) * cw]
            ui_ref[pl.ds(m1 + t, n2, stride=pitch), :] = p[n2:, t * cw:(t + 1) * cw]

    def step_a(it, carry):
        slab_group(it * group, group)
        return carry

    lax.fori_loop(0, n1 // group, step_a, 0, unroll=4)
    if n1 % group:
        slab_group((n1 // group) * group, n1 % group)

    br = br_ref[...]
    bi = bi_ref[...]

    @pl.when(step > 0)
    def _():
        prev_c = jnp.where(cidx == 0, ncb - 1, cidx - 1)
        prev_b = jnp.where(cidx == 0, bidx - 1, bidx)
        writeback(prev_b, prev_c, False)

    def step_c(j, carry):
        tr = twr_ref[j]
        ti = twi_ref[j]
        fm = jnp.concatenate([br * tr - bi * ti, br * ti + bi * tr], axis=0).astype(BF16)
        start = pl.multiple_of(j * pitch, 8)
        rhs = jnp.concatenate([ur_ref[pl.ds(start, kc), :], ui_ref[pl.ds(start, kc), :]],
                              axis=1).astype(BF16)
        r = jnp.dot(fm, rhs, preferred_element_type=F32)
        vr_ref[pl.ds(j, n1, stride=n2), :] = r[:n1, :cw] - r[kc:kc + n1, cw:]
        vi_ref[pl.ds(j, n1, stride=n2), :] = r[:n1, cw:] + r[kc:kc + n1, :cw]
        return carry

    lax.fori_loop(0, n2, step_c, 0, unroll=4)

    writeback(bidx, cidx, True)

    @pl.when(step == last)
    def _():
        writeback(bidx, cidx, False)


def _seqdft(four):
    b, length, c = four.shape
    n1, n2, kc, pitch, rows, cw, group = _dft_plan(length, c)
    tables = _dft_tables(length, n1, n2, kc)
    out = jax.ShapeDtypeStruct((b, length, c), F32)
    return pl.pallas_call(
        functools.partial(_seqdft_body, n1=n1, n2=n2, kc=kc, pitch=pitch, cw=cw, group=group),
        grid=(b, c // cw),
        in_specs=[pl.BlockSpec((1, length, cw), lambda bi, ci: (bi, 0, ci))]
                 + [_resident(t.shape) for t in tables],
        out_specs=[pl.BlockSpec(memory_space=pl.ANY), pl.BlockSpec(memory_space=pl.ANY)],
        out_shape=[out, out],
        scratch_shapes=[pltpu.VMEM((rows, cw), F32), pltpu.VMEM((rows, cw), F32),
                        pltpu.VMEM((length, cw), F32), pltpu.VMEM((length, cw), F32),
                        pltpu.SemaphoreType.DMA((2,))],
        compiler_params=_cparams(("arbitrary", "arbitrary")),
        name="seqdft",
    )(four, *tables)


def _outproj_body(o_ref, g_ref, vr_ref, vi_ref, x_ref, gw_ref, cw_ref, sw_ref, wo_ref, n2w_ref, wr_ref, br_ref,
                  base_in_ref, h_ref, hn_ref, route_ref, cnt_ref, base_ref, *, fnorm):
    first = jnp.logical_and(pl.program_id(0) == 0, pl.program_id(1) == 0)

    @pl.when(first)
    def _():
        base_ref[...] = base_in_ref[...]

    tm = o_ref.shape[1]
    parts = []
    for h in range(GLA_HEADS):
        vs = slice(h * HEAD_V, (h + 1) * HEAD_V)
        o = o_ref[0, :, vs]
        o = o * lax.rsqrt(jnp.mean(o * o, axis=-1, keepdims=True) + EPS)
        o = o * gw_ref[:, vs]
        g = g_ref[0, :, vs]
        parts.append((o * (g * (1.0 / (1.0 + jnp.exp(-g))))).astype(BF16))
    cw = cw_ref[...].astype(BF16)
    sw = sw_ref[...].astype(BF16)
    for gi in range(FNET_WIDTH // FNET_GROUP_W):
        cs = slice(gi * FNET_GROUP_W, (gi + 1) * FNET_GROUP_W)
        vr = vr_ref[0, :, cs].astype(BF16)
        vi = vi_ref[0, :, cs].astype(BF16)
        f = (jnp.dot(vr, cw, preferred_element_type=F32) + jnp.dot(vi, sw, preferred_element_type=F32))
        parts.append((f * fnorm).astype(BF16))
    merged = jnp.concatenate(parts, axis=-1)
    h1 = x_ref[0] + jnp.dot(merged, wo_ref[...], preferred_element_type=F32)
    h_ref[0] = h1
    hn = h1 * lax.rsqrt(jnp.mean(h1 * h1, axis=-1, keepdims=True) + EPS) * n2w_ref[...]
    hn_ref[0] = hn

    hn_hi = hn.astype(BF16)
    hn_lo = (hn - hn_hi.astype(F32)).astype(BF16)
    pp = jnp.dot(jnp.concatenate([hn_hi, hn_lo], axis=0), wr_ref[...], preferred_element_type=F32)
    logits = ((pp[:tm, :LANES] + pp[:tm, LANES:]) + (pp[tm:, :LANES] + pp[tm:, LANES:])) + br_ref[...]
    lane = lax.broadcasted_iota(jnp.int32, (tm, LANES), 1)
    neg = jnp.float32(-jnp.inf)
    gl = jnp.where(lane < N_GROUPS, logits, neg)
    gmax = jnp.max(gl, axis=-1, keepdims=True)
    grp = jnp.min(jnp.where(gl == gmax, lane, LANES), axis=-1, keepdims=True)
    grp_w = 1.0 / jnp.sum(jnp.exp(gl - gmax), axis=-1, keepdims=True)
    elane = lane - N_GROUPS
    member = jnp.logical_and(jnp.logical_and(elane >= 0, elane < N_EXPERTS),
                             (elane >> 3) == grp)
    el = jnp.where(member, logits, neg)
    m0 = jnp.max(el, axis=-1, keepdims=True)
    i0 = jnp.min(jnp.where(el == m0, lane, LANES), axis=-1, keepdims=True)
    el2 = jnp.where(lane == i0, neg, el)
    m1 = jnp.max(el2, axis=-1, keepdims=True)
    i1 = jnp.min(jnp.where(el2 == m1, lane, LANES), axis=-1, keepdims=True)
    p1 = jnp.exp(m1 - m0)
    den = 1.0 + p1
    w0 = (1.0 / den) * grp_w
    w1 = (p1 / den) * grp_w
    e0 = i0 - N_GROUPS
    e1 = i1 - N_GROUPS

    oh0 = (lane == e0).astype(BF16)
    oh1 = (lane == e1).astype(BF16)
    r_i = lax.broadcasted_iota(jnp.int32, (tm, tm), 0)
    c_i = lax.broadcasted_iota(jnp.int32, (tm, tm), 1)
    ltri = (c_i < r_i).astype(BF16)
    c0 = jnp.dot(ltri, oh0, preferred_element_type=F32)
    c1 = jnp.dot(ltri, oh1, preferred_element_type=F32)
    oh0f = oh0.astype(F32)
    oh1f = oh1.astype(F32)
    tot0 = jnp.sum(oh0f, axis=0, keepdims=True)
    tot1 = jnp.sum(oh1f, axis=0, keepdims=True)
    base = base_ref[...]
    rank0 = jnp.sum(oh0f * (c0 + base), axis=-1, keepdims=True)
    rank1 = jnp.sum(oh1f * (c1 + base + tot0), axis=-1, keepdims=True)
    new_base = base + tot0 + tot1
    base_ref[...] = new_base
    cnt_ref[...] = new_base

    route = jnp.where(lane == 0, e0.astype(F32),
            jnp.where(lane == 1, e1.astype(F32),
            jnp.where(lane == 2, w0,
            jnp.where(lane == 3, w1,
            jnp.where(lane == 4, rank0,
            jnp.where(lane == 5, rank1, 0.0))))))
    route_ref[0] = route


def _outproj(o, g, vr, vi, x, gw, cw, sw, wo, n2w, wr, br, base_in, length):
    b, s, d = x.shape
    tm = _row_tile(s, 512)
    fnorm = 1.0 / math.sqrt(length * FNET_GROUP_W)
    row = lambda w: pl.BlockSpec((1, tm, w), lambda bi, i: (bi, i, 0))
    return pl.pallas_call(
        functools.partial(_outproj_body, fnorm=fnorm),
        grid=(b, s // tm),
        in_specs=[row(GLA_WIDTH), row(GLA_WIDTH), row(FNET_WIDTH), row(FNET_WIDTH), row(d),
                  _resident(gw.shape), _resident(cw.shape), _resident(sw.shape), _resident(wo.shape),
                  _resident(n2w.shape), _resident(wr.shape), _resident(br.shape), _resident(base_in.shape)],
        out_specs=[row(d), row(d), row(LANES), pl.BlockSpec((1, LANES), lambda bi, i: (0, 0))],
        out_shape=[jax.ShapeDtypeStruct((b, s, d), F32), jax.ShapeDtypeStruct((b, s, d), F32),
                   jax.ShapeDtypeStruct((b, s, LANES), F32), jax.ShapeDtypeStruct((1, LANES), F32)],
        scratch_shapes=[pltpu.VMEM((1, LANES), F32)],
        compiler_params=_cparams(("arbitrary", "arbitrary")),
        name="outproj_router",
    )(o, g, vr, vi, x, gw, cw, sw, wo, n2w, wr, br, base_in)


DISPATCH_WINDOW = 256

def _dispatch_body(slot_ref, zblk_ref, zval_ref, *refs, counts, win):
    srcs = refs[:len(counts)]
    xs_out, zero_ref, tile_ref, zsem, lsems, ssems = refs[len(counts):]
    step = pl.program_id(0)
    nsteps = sum(counts) // win

    @pl.when(step == 0)
    def _():
        zero_ref[...] = jnp.zeros(zero_ref.shape, zero_ref.dtype)

        def zcopy(i):
            row = pl.multiple_of(zblk_ref[i] * MOE_BM, MOE_BM)
            return pltpu.make_async_copy(zero_ref, xs_out.at[pl.ds(row, MOE_BM)], zsem)

        for i in range(2 * N_EXPERTS):
            @pl.when(zval_ref[i] != 0)
            def _():
                zcopy(i).start()
        for i in range(2 * N_EXPERTS):
            @pl.when(zval_ref[i] != 0)
            def _():
                zcopy(i).wait()

    def load(w, start):
        lo = 0
        for src, n in zip(srcs, counts):
            hi = lo + n // win

            @pl.when(jnp.logical_and(w >= lo, w < hi))
            def _(src=src, lo=lo):
                row0 = pl.multiple_of((w - lo) * win, 8)
                b = lax.rem(w, 3)
                cp = pltpu.make_async_copy(src.at[pl.ds(row0, win)], tile_ref.at[b], lsems.at[b])
                if start:
                    cp.start()
                else:
                    cp.wait()

            lo = hi

    def scatter(w, start):
        b = lax.rem(w, 3)
        sem = ssems.at[lax.rem(w, 2)]
        tok0 = w * win

        def body(r, c):
            for k in range(TOP_K):
                s = slot_ref[2 * (tok0 + r) + k]
                cp = pltpu.make_async_copy(tile_ref.at[b, pl.ds(r, 1)], xs_out.at[pl.ds(s, 1)], sem)
                if start:
                    cp.start(priority=k)
                else:
                    cp.wait()
            return c

        lax.fori_loop(0, win, body, 0, unroll=8)

    @pl.when(step == 0)
    def _():
        load(step, True)

    @pl.when(step + 1 < nsteps)
    def _():
        load(step + 1, True)

    load(step, False)
    scatter(step, True)

    @pl.when(step > 0)
    def _():
        scatter(step - 1, False)

    @pl.when(step == nsteps - 1)
    def _():
        scatter(step, False)


def _dispatch(slots, zblk, zval, sources, n_rows):
    d = sources[0].shape[1]
    dtype = sources[0].dtype
    counts = tuple(s.shape[0] for s in sources)
    win = functools.reduce(math.gcd, counts + (DISPATCH_WINDOW,))
    return pl.pallas_call(
        functools.partial(_dispatch_body, counts=counts, win=win),
        grid_spec=pltpu.PrefetchScalarGridSpec(
            num_scalar_prefetch=3,
            grid=(sum(counts) // win,),
            in_specs=[pl.BlockSpec(memory_space=pl.ANY) for _ in sources],
            out_specs=pl.BlockSpec(memory_space=pl.ANY),
            scratch_shapes=[pltpu.VMEM((MOE_BM, d), dtype), pltpu.VMEM((3, win, d), dtype),
                            pltpu.SemaphoreType.DMA(()), pltpu.SemaphoreType.DMA((3,)),
                            pltpu.SemaphoreType.DMA((2,))]),
        out_shape=jax.ShapeDtypeStruct((n_rows, d), dtype),
        compiler_params=_cparams(("arbitrary",)),
        name="moe_dispatch",
    )(slots, zblk, zval, *sources)


def _experts_body(be_ref, nu_ref, x_ref, wg_ref, wu_ref, wd_ref, y_ref, wgb_ref, wub_ref, wdb_ref):
    i = pl.program_id(0)
    used = i < nu_ref[0]
    ic = jnp.minimum(i, nu_ref[0] - 1)
    fresh = jnp.logical_or(i == 0, be_ref[ic] != be_ref[jnp.maximum(ic - 1, 0)])

    @pl.when(jnp.logical_and(used, fresh))
    def _():
        wgb_ref[...] = wg_ref[0].astype(BF16)
        wub_ref[...] = wu_ref[0].astype(BF16)
        wdb_ref[...] = wd_ref[0].astype(BF16)

    @pl.when(used)
    def _():
        xb = x_ref[...].astype(BF16)
        gate = jnp.dot(xb, wgb_ref[...], preferred_element_type=F32)
        up = jnp.dot(xb, wub_ref[...], preferred_element_type=F32)
        hid = (gate * (1.0 / (1.0 + jnp.exp(-gate))) * up).astype(BF16)
        y_ref[...] = jnp.dot(hid, wdb_ref[...], preferred_element_type=F32)

    @pl.when(jnp.logical_not(used))
    def _():
        y_ref[...] = jnp.zeros(y_ref.shape, F32)


def _experts(block_expert, n_used, xs, wg, wu, wd):
    p, d = xs.shape
    nb = p // MOE_BM
    blk = lambda i, be, nu: (jnp.minimum(i, nu[0] - 1), 0)
    wsel = lambda i, be, nu: (be[jnp.minimum(i, nu[0] - 1)], 0, 0)
    return pl.pallas_call(
        _experts_body,
        grid_spec=pltpu.PrefetchScalarGridSpec(
            num_scalar_prefetch=2,
            grid=(nb,),
            in_specs=[pl.BlockSpec((MOE_BM, d), blk),
                      pl.BlockSpec((1, d, EXPERT_FF), wsel),
                      pl.BlockSpec((1, d, EXPERT_FF), wsel),
                      pl.BlockSpec((1, EXPERT_FF, d), wsel)],
            out_specs=pl.BlockSpec((MOE_BM, d), lambda i, be, nu: (i, 0)),
            scratch_shapes=[pltpu.VMEM((d, EXPERT_FF), BF16), pltpu.VMEM((d, EXPERT_FF), BF16),
                            pltpu.VMEM((EXPERT_FF, d), BF16)]),
        out_shape=jax.ShapeDtypeStruct((p, d), F32),
        compiler_params=_cparams(("arbitrary",)),
        name="moe_experts",
    )(block_expert, n_used, xs, wg, wu, wd)


def _combine_body(slot_ref, h_ref, route_ref, ys_any, fw_ref, o_ref, ybuf_ref, sems, *, nt):
    tm = h_ref.shape[0]
    i = pl.program_id(0)

    def copy(tile, r, k):
        buf = lax.rem(tile, 2)
        s = slot_ref[2 * (tile * tm + r) + k]
        return pltpu.make_async_copy(ys_any.at[pl.ds(s, 1)], ybuf_ref.at[buf, k, pl.ds(r, 1)], sems.at[buf])

    def issue_tile(tile):
        def body(r, carry):
            copy(tile, r, 0).start()
            copy(tile, r, 1).start()
            return carry
        lax.fori_loop(0, tm, body, 0, unroll=8)

    @pl.when(i == 0)
    def _():
        issue_tile(i)

    @pl.when(i + 1 < nt)
    def _():
        issue_tile(i + 1)

    def drain(r, carry):
        copy(i, r, 0).wait()
        copy(i, r, 1).wait()
        return carry

    lax.fori_loop(0, tm, drain, 0, unroll=8)

    buf = lax.rem(i, 2)
    w0 = route_ref[:, 2:3]
    w1 = route_ref[:, 3:4]
    h = h_ref[...] + (ybuf_ref[buf, 0] * w0 + ybuf_ref[buf, 1] * w1)
    o_ref[...] = h * lax.rsqrt(jnp.mean(h * h, axis=-1, keepdims=True) + EPS) * fw_ref[...]


def _combine(slots, h2d, route2d, ys, fw):
    t, d = h2d.shape
    tm = _row_tile(t, 256)
    nt = t // tm
    return pl.pallas_call(
        functools.partial(_combine_body, nt=nt),
        grid_spec=pltpu.PrefetchScalarGridSpec(
            num_scalar_prefetch=1,
            grid=(nt,),
            in_specs=[pl.BlockSpec((tm, d), lambda i, sl: (i, 0)),
                      pl.BlockSpec((tm, LANES), lambda i, sl: (i, 0)),
                      pl.BlockSpec(memory_space=pl.ANY),
                      pl.BlockSpec((1, d), lambda i, sl: (0, 0))],
            out_specs=pl.BlockSpec((tm, d), lambda i, sl: (i, 0)),
            scratch_shapes=[pltpu.VMEM((2, TOP_K, tm, d), F32), pltpu.SemaphoreType.DMA((2,))]),
        out_shape=jax.ShapeDtypeStruct((t, d), F32),
        compiler_params=_cparams(("arbitrary",)),
        name="moe_combine",
    )(slots, h2d, route2d, ys, fw)


def _prep_weights(norm1_w, w_in, w_gate_up, b_gate_up, gla_norm_w, w_out, norm2_w, w_router_group,
                  b_router_group, w_router_expert, b_router_expert, w_expert_gate, w_expert_up,
                  w_expert_down, final_norm_w):
    w = w_in[0]
    wz = jnp.pad(w[:, OFF_ZF:OFF_F], ((0, 0), (0, LANES - 2 * GATE_RANK)))
    wg_f = jnp.pad(w_gate_up[0, 0], ((0, LANES - GATE_RANK), (0, 0)))
    wg_b = jnp.pad(w_gate_up[0, 1], ((GATE_RANK, LANES - 2 * GATE_RANK), (0, 0)))
    rpad = LANES - N_GROUPS - N_EXPERTS
    wr = jnp.pad(jnp.concatenate([w_router_group[0], w_router_expert[0]], axis=1), ((0, 0), (0, rpad)))
    br = jnp.pad(jnp.concatenate([b_router_group[0], b_router_expert[0]]), (0, rpad))[None]
    k = np.arange(FNET_GROUP_W)
    ang = 2.0 * math.pi * ((k[:, None] * k[None, :]) % FNET_GROUP_W) / FNET_GROUP_W
    return dict(
        nw1=norm1_w[0][None], wqk=w[:, :OFF_V].astype(BF16), wv=w[:, OFF_V:OFF_G].astype(BF16), wg=w[:, OFF_G:OFF_ZF].astype(BF16),
        wz=wz.astype(BF16), wf=w[:, OFF_F:].astype(BF16),
        wg_f=_split_bf16(wg_f), wg_b=_split_bf16(wg_b), bg_f=b_gate_up[0, 0][None], bg_b=b_gate_up[0, 1][None],
        gw=gla_norm_w[0][None], cw=jnp.asarray(np.cos(ang), F32), sw=jnp.asarray(np.sin(ang), F32),
        wo=w_out[0].astype(BF16), nw2=norm2_w[0][None], wr=_split_bf16(wr), br=br,
        weg=w_expert_gate[0], weu=w_expert_up[0], wed=w_expert_down[0],
        fw=final_norm_w[None])


def _split_bf16(w):
    hi = w.astype(BF16)
    lo = (w - hi.astype(F32)).astype(BF16)
    return jnp.concatenate([hi, lo], axis=1)


def _mixer_and_route(x, meta, pw, base_in):
    b, s, d = x.shape
    length = s + N_META
    qk, v, g, z, four = _inproj(x, meta["four"], pw["nw1"], pw["wqk"], pw["wv"], pw["wg"], pw["wz"], pw["wf"])
    s0 = _meta_state(meta["k"], meta["v"], meta["z"], pw["wg_f"], pw["bg_f"])
    o_f = _gla(qk, v, z, pw["wg_f"], pw["bg_f"], s0, None, reverse=False)
    o = _gla(qk, v, z, pw["wg_b"], pw["bg_b"], jnp.zeros_like(s0), o_f, reverse=True)
    vr, vi = _seqdft(four)
    return _outproj(o, g, vr, vi, x, pw["gw"], pw["cw"], pw["sw"], pw["wo"], pw["nw2"], pw["wr"], pw["br"],
                    base_in, length)


def kernel(x_prompt, x_sample, meta_tokens, norm1_w, w_in, w_gate_up, b_gate_up, gla_norm_w, w_out, norm2_w,
           w_router_group, b_router_group, w_router_expert, b_router_expert, w_expert_gate, w_expert_up,
           w_expert_down, final_norm_w):
    pw = _prep_weights(norm1_w, w_in, w_gate_up, b_gate_up, gla_norm_w, w_out, norm2_w, w_router_group,
                       b_router_group, w_router_expert, b_router_expert, w_expert_gate, w_expert_up,
                       w_expert_down, final_norm_w)
    qk_m, v_m, _, z_m, four_m = _inproj(meta_tokens[None], None, pw["nw1"], pw["wqk"], pw["wv"], pw["wg"],
                                         pw["wz"], pw["wf"])
    meta = dict(k=qk_m[0, :, GLA_KEY:], v=v_m[0], z=z_m[0], four=four_m)

    xs_in = (x_prompt, x_sample)
    base = jnp.zeros((1, LANES), F32)
    mixed = []
    for x in xs_in:
        h, hn, route, base = _mixer_and_route(x, meta, pw, base)
        mixed.append((h, hn, route))

    counts = base[0, :N_EXPERTS].astype(jnp.int32)
    padded = (counts + MOE_BM - 1) // MOE_BM * MOE_BM
    pends = jnp.cumsum(padded)
    pstart = (pends - padded).astype(F32)
    n_assign = sum(x.shape[0] * x.shape[1] for x in xs_in) * TOP_K
    nb = -(-(n_assign + N_EXPERTS * (MOE_BM - 1)) // MOE_BM)
    n_used = (pends[-1:] // MOE_BM).astype(jnp.int32)
    blk_row = jnp.arange(nb, dtype=jnp.int32) * MOE_BM
    block_expert = jnp.minimum(jnp.sum((pends[None, :] <= blk_row[:, None]).astype(jnp.int32), axis=1),
                               N_EXPERTS - 1)
    tail = n_used[0] + jnp.arange(N_EXPERTS, dtype=jnp.int32)
    zblk = jnp.concatenate([jnp.maximum(pends // MOE_BM - 1, 0), jnp.minimum(tail, nb - 1)]).astype(jnp.int32)
    zval = jnp.concatenate([padded > 0, tail < nb]).astype(jnp.int32)

    slot_list = []
    for (h, hn, route) in mixed:
        t = h.shape[0] * h.shape[1]
        r2 = route.reshape(t, LANES)
        e = r2[:, 0:2].astype(jnp.int32)
        onehot = (e[:, :, None] == jnp.arange(N_EXPERTS, dtype=jnp.int32)).astype(F32)
        slots = (jnp.sum(onehot * pstart, axis=-1) + r2[:, 4:6]).astype(jnp.int32).reshape(-1)
        slot_list.append(slots)
    xs = _dispatch(jnp.concatenate(slot_list), zblk, zval,
                   [hn.reshape(-1, hn.shape[-1]) for (_, hn, _) in mixed], nb * MOE_BM)

    ys = _experts(block_expert, n_used, xs, pw["weg"], pw["weu"], pw["wed"])

    outs = []
    for x, (h, hn, route), slots in zip(xs_in, mixed, slot_list):
        t = h.shape[0] * h.shape[1]
        y = _combine(slots, h.reshape(t, D_MODEL), route.reshape(t, LANES), ys, pw["fw"])
        outs.append(y.reshape(x.shape))
    return tuple(outs)
```
